```python
import jax, jax.numpy as jnp
from jax import lax
import numpy as np

D_MODEL = 2048
BATCH = 32
SEQ = 256
DEPTH = 2
DEC_BATCH = 8
DEC_SEQ = 2048
PAST_LEN = 512

GRID_W = 64
HEAD_DIM = 128
MIX_W = D_MODEL
N_HEADS = (MIX_W // 2) // HEAD_DIM
N_KV = 2
GROUP = N_HEADS // N_KV
ATTN_W = N_HEADS * HEAD_DIM
KV_W = N_KV * HEAD_DIM
SGU_GROUPS = 4
SGU_W = MIX_W // 4
SGU_GD = SGU_W // SGU_GROUPS
FNET_GROUPS = 4
FNET_W = MIX_W // 4
FNET_GD = FNET_W // FNET_GROUPS
PROJ_W = ATTN_W + 2 * KV_W + 2 * SGU_W + FNET_W
PROJ_SPLITS = (ATTN_W, ATTN_W + KV_W, ATTN_W + 2 * KV_W,
               ATTN_W + 2 * KV_W + SGU_W, ATTN_W + 2 * KV_W + 2 * SGU_W)
BLOCK = 128
WINDOW = 128
CHUNK = 128
ROPE_BASE = 10000.0
N_MOD = 6
PEER_HEADS = 8
PEER_NKEYS = 128
PEER_N = PEER_NKEYS * PEER_NKEYS
PEER_TOPK = 16
PEER_QDIM = 128
PEER_HALF = PEER_QDIM // 2
PEER_CHUNK = 128
DN_ALPHA = (2 * DEPTH) ** 0.25
DN_BETA = (8 * DEPTH) ** -0.25
LN_EPS = 1e-5
NEG_INF = -1e30

kernel_name = "hybrid_diffusion_trunk_step"


def layer_norm(x, g=None, b=None):
    xf = x.astype(jnp.float32)
    mu = jnp.mean(xf, axis=-1, keepdims=True)
    var = jnp.mean(jnp.square(xf - mu), axis=-1, keepdims=True)
    y = (xf - mu) * lax.rsqrt(var + LN_EPS)
    if g is not None:
        y = y * g.astype(jnp.float32) + b.astype(jnp.float32)
    return y.astype(x.dtype)


def modulation(cvec, w_mod, b_mod):
    m = jax.nn.silu(cvec) @ w_mod + b_mod
    return jnp.split(m[:, None, :], N_MOD, axis=-1)


def axial_tables(S):
    rows = S // GRID_W
    r = jnp.repeat(jnp.arange(rows), GRID_W).astype(jnp.float32)
    col = (jnp.arange(S) % GRID_W).astype(jnp.float32)
    n = HEAD_DIM // 4
    inv = ROPE_BASE ** (-jnp.arange(n, dtype=jnp.float32) / n)
    ang_r = (r[:, None] * inv)[:, None, :]
    ang_c = (col[:, None] * inv)[:, None, :]
    return jnp.cos(ang_r), jnp.sin(ang_r), jnp.cos(ang_c), jnp.sin(ang_c)


def _rot_half(t, cos, sin):
    t1, t2 = jnp.split(t, 2, axis=-1)
    return jnp.concatenate([t1 * cos - t2 * sin, t2 * cos + t1 * sin], axis=-1)


def axial_rope(x, tables):
    cos_r, sin_r, cos_c, sin_c = tables
    xf = x.astype(jnp.float32)
    xr, xc = jnp.split(xf, 2, axis=-1)
    y = jnp.concatenate([_rot_half(xr, cos_r, sin_r), _rot_half(xc, cos_c, sin_c)], axis=-1)
    return y.astype(x.dtype)


def context_attention(q, k, v, sink):
    B, L = q.shape[:2]
    qg = q.reshape(B, L, N_KV, GROUP, HEAD_DIM)
    s = jnp.einsum('bqhgd,bkhd->bhgqk', qg, k, preferred_element_type=jnp.float32) * (HEAD_DIM ** -0.5)
    s_sink = jnp.broadcast_to(sink.astype(jnp.float32).reshape(1, N_KV, GROUP, 1, 1), s.shape[:-1] + (1,))
    p = jax.nn.softmax(jnp.concatenate([s, s_sink], axis=-1), axis=-1)[..., :L]
    o = jnp.einsum('bhgqk,bkhd->bqhgd', p.astype(v.dtype), v)
    return o.reshape(B, L, ATTN_W)


def latent_attention(q, k, v, ck, cv, sink):
    B, S = q.shape[:2]
    nb = S // BLOCK
    scale = HEAD_DIM ** -0.5
    qb = q.reshape(B, nb, BLOCK, N_KV, GROUP, HEAD_DIM)

    def band(t):
        tp = jnp.pad(t, ((0, 0), (BLOCK, BLOCK), (0, 0), (0, 0))).reshape(B, nb + 2, BLOCK, N_KV, HEAD_DIM)
        return jnp.concatenate([tp[:, :-2], tp[:, 1:-1], tp[:, 2:]], axis=2)

    kb, vb = band(k), band(v)
    s_loc = jnp.einsum('bnqhgd,bnmhd->bnhgqm', qb, kb, preferred_element_type=jnp.float32) * scale
    blk = jnp.arange(nb)[:, None, None] * BLOCK
    qpos = blk + jnp.arange(BLOCK)[None, :, None]
    kpos = blk - BLOCK + jnp.arange(3 * BLOCK)[None, None, :]
    valid = (jnp.abs(kpos - qpos) <= WINDOW) & (kpos >= 0) & (kpos < S)
    s_loc = jnp.where(valid[None, :, None, None], s_loc, NEG_INF)
    s_ctx = jnp.einsum('bnqhgd,bchd->bnhgqc', qb, ck, preferred_element_type=jnp.float32) * scale
    s_sink = jnp.broadcast_to(sink.astype(jnp.float32).reshape(1, 1, N_KV, GROUP, 1, 1), s_loc.shape[:-1] + (1,))
    p = jax.nn.softmax(jnp.concatenate([s_loc, s_ctx, s_sink], axis=-1), axis=-1)
    nloc = 3 * BLOCK
    lc = ck.shape[1]
    p_loc = p[..., :nloc].astype(v.dtype)
    p_ctx = p[..., nloc:nloc + lc].astype(v.dtype)
    o = (jnp.einsum('bnhgqm,bnmhd->bnqhgd', p_loc, vb)
         + jnp.einsum('bnhgqc,bchd->bnqhgd', p_ctx, cv))
    return o.reshape(B, S, ATTN_W)


def spatial_gating(u, v, g, w_s, b_s):
    B, S, _ = u.shape
    n = S // CHUNK
    u = jax.nn.gelu(u, approximate=False)
    v = jax.nn.gelu(v, approximate=False).reshape(B, n, CHUNK, SGU_GROUPS, SGU_GD)
    vn = layer_norm(v) * g.reshape(SGU_GROUPS, SGU_GD)
    mixed = jnp.einsum('gpq,bnqgc->bnpgc', w_s, vn) + b_s.T[:, :, None]
    return u * mixed.reshape(B, S, SGU_W)


def fourier_mix(f):
    B, S, _ = f.shape
    ff = f.astype(jnp.float32).reshape(B, S, FNET_GROUPS, FNET_GD)
    y = jnp.fft.fft2(ff, axes=(1, 3), norm='ortho').real
    return y.reshape(B, S, FNET_W).astype(f.dtype)


def peer(h, wq, subkeys, u_tab, v_tab):
    B, S, D = h.shape
    T = B * S
    hf = h.reshape(T, D)
    q = (hf @ wq).reshape(T, PEER_HEADS, 2, PEER_HALF)
    s = jnp.einsum('thpk,hpnk->thpn', q, subkeys, preferred_element_type=jnp.float32)
    s1, i1 = lax.top_k(s[:, :, 0], PEER_TOPK)
    s2, i2 = lax.top_k(s[:, :, 1], PEER_TOPK)
    cand = (s1[..., :, None] + s2[..., None, :]).reshape(T, PEER_HEADS, PEER_TOPK * PEER_TOPK)
    cidx = (i1[..., :, None] * PEER_NKEYS + i2[..., None, :]).reshape(T, PEER_HEADS, PEER_TOPK * PEER_TOPK)
    top_s, pos = lax.top_k(cand, PEER_TOPK)
    idx = jnp.take_along_axis(cidx, pos, axis=-1)
    gate = jax.nn.softmax(top_s, axis=-1)
    nc = T // PEER_CHUNK

    def apply(args):
        xc, ic, gc = args
        a = jax.nn.gelu(jnp.einsum('cd,chkd->chk', xc, u_tab[ic], preferred_element_type=jnp.float32),
                        approximate=False)
        w = (gc * a).astype(xc.dtype)
        return jnp.einsum('chk,chkd->cd', w, v_tab[ic])

    out = lax.map(apply, (hf.reshape(nc, PEER_CHUNK, D),
                          idx.reshape(nc, PEER_CHUNK, PEER_HEADS, PEER_TOPK),
                          gate.reshape(nc, PEER_CHUNK, PEER_HEADS, PEER_TOPK)))
    return out.reshape(B, S, D)


def block(x, mod, w_in, sink, sgu_g, sgu_w, sgu_b, w_out, ln1_g, ln1_b,
          peer_wq, peer_keys, peer_u, peer_v, ln2_g, ln2_b, ctx_kv, rope):
    sh_a, sc_a, g_a, sh_f, sc_f, g_f = mod
    B, S, _ = x.shape
    h = layer_norm(x) * (1 + sc_a) + sh_a
    q, k, v, ug, vg, f = jnp.split(h @ w_in, list(PROJ_SPLITS), axis=-1)
    q = q.reshape(B, S, N_HEADS, HEAD_DIM)
    k = k.reshape(B, S, N_KV, HEAD_DIM)
    v = v.reshape(B, S, N_KV, HEAD_DIM)
    if ctx_kv is None:
        attn = context_attention(q, k, v, sink)
        new_kv = (k, v)
    else:
        attn = latent_attention(axial_rope(q, rope), axial_rope(k, rope), v, ctx_kv[0], ctx_kv[1], sink)
        new_kv = None
    mix = jnp.concatenate([attn, spatial_gating(ug, vg, sgu_g, sgu_w, sgu_b), fourier_mix(f)], axis=-1) @ w_out
    x = layer_norm(DN_ALPHA * x + g_a * mix, ln1_g, ln1_b)
    h = layer_norm(x) * (1 + sc_f) + sh_f
    x = layer_norm(DN_ALPHA * x + g_f * peer(h, peer_wq, peer_keys, peer_u, peer_v), ln2_g, ln2_b)
    return x, new_kv


def setup_inputs(seed: int = 0) -> dict:
    key = jax.random.key(seed)
    ks = jax.random.split(key, 24)
    f32 = jnp.float32

    def nrm(k, shape, scale=1.0):
        return jax.random.normal(k, shape, f32) * scale

    return {
        'x_prompt': nrm(ks[0], (BATCH, SEQ, D_MODEL)),
        'x_sample': nrm(ks[1], (DEC_BATCH, DEC_SEQ, D_MODEL)),
        'cache_k': nrm(ks[2], (DEC_BATCH, DEPTH, PAST_LEN, N_KV, HEAD_DIM)),
        'cache_v': nrm(ks[3], (DEC_BATCH, DEPTH, PAST_LEN, N_KV, HEAD_DIM)),
        'c': nrm(ks[4], (DEC_BATCH, D_MODEL)),
        'c_ctx': nrm(ks[5], (D_MODEL,)),
        'w_mod': nrm(ks[6], (DEPTH, D_MODEL, N_MOD * D_MODEL), 0.5 * D_MODEL ** -0.5),
        'b_mod': nrm(ks[7], (DEPTH, N_MOD * D_MODEL), 0.02),
        'w_in': nrm(ks[8], (DEPTH, D_MODEL, PROJ_W), D_MODEL ** -0.5),
        'attn_sink': nrm(ks[9], (DEPTH, N_HEADS), 0.5),
        'sgu_g': 1.0 + nrm(ks[10], (DEPTH, SGU_W), 0.02),
        'sgu_w': nrm(ks[11], (DEPTH, SGU_GROUPS, CHUNK, CHUNK), CHUNK ** -0.5),
        'sgu_b': nrm(ks[12], (DEPTH, SGU_GROUPS, CHUNK), 0.02),
        'w_out': nrm(ks[13], (DEPTH, MIX_W, D_MODEL), DN_BETA * MIX_W ** -0.5),
        'ln1_g': 1.0 + nrm(ks[14], (DEPTH, D_MODEL), 0.02),
        'ln1_b': nrm(ks[15], (DEPTH, D_MODEL), 0.02),
        'peer_wq': nrm(ks[16], (DEPTH, D_MODEL, PEER_HEADS * PEER_QDIM), D_MODEL ** -0.5),
        'peer_keys': nrm(ks[17], (DEPTH, PEER_HEADS, 2, PEER_NKEYS, PEER_HALF), PEER_HALF ** -0.5),
        'peer_u': nrm(ks[18], (DEPTH, PEER_N, D_MODEL), D_MODEL ** -0.5),
        'peer_v': nrm(ks[19], (DEPTH, PEER_N, D_MODEL), DN_BETA),
        'ln2_g': 1.0 + nrm(ks[20], (DEPTH, D_MODEL), 0.02),
        'ln2_b': nrm(ks[21], (DEPTH, D_MODEL), 0.02),
    }


def reference(x_prompt, x_sample, cache_k, cache_v, c, c_ctx, w_mod, b_mod, w_in, attn_sink,
              sgu_g, sgu_w, sgu_b, w_out, ln1_g, ln1_b, peer_wq, peer_keys, peer_u, peer_v,
              ln2_g, ln2_b):
    rope = axial_tables(x_sample.shape[1])
    xp, xs = x_prompt, x_sample
    new_k, new_v = [], []
    for l in range(DEPTH):
        lw = (w_in[l], attn_sink[l], sgu_g[l], sgu_w[l], sgu_b[l], w_out[l], ln1_g[l], ln1_b[l],
              peer_wq[l], peer_keys[l], peer_u[l], peer_v[l], ln2_g[l], ln2_b[l])
        mod_ctx = modulation(c_ctx[None, :], w_mod[l], b_mod[l])
        mod_lat = modulation(c, w_mod[l], b_mod[l])
        xp, (kl, vl) = block(xp, mod_ctx, *lw, None, None)
        new_k.append(kl)
        new_v.append(vl)
        xs, _ = block(xs, mod_lat, *lw, (cache_k[:, l], cache_v[:, l]), rope)
    new_cache_k = jnp.stack(new_k, axis=1)
    new_cache_v = jnp.stack(new_v, axis=1)
    return (xp, xs, new_cache_k, new_cache_v)
```

```python
import functools

import jax
import jax.numpy as jnp
from jax import lax
from jax.experimental import pallas as pl
from jax.experimental.pallas import tpu as pltpu

D_MODEL = 2048
DEPTH = 2
GRID_W = 64
HEAD_DIM = 128
N_HEADS = 8
N_KV = 2
GROUP = N_HEADS // N_KV
ATTN_W = N_HEADS * HEAD_DIM
KV_W = N_KV * HEAD_DIM
SGU_GROUPS = 4
SGU_W = 512
FNET_GROUPS = 4
FNET_W = 512
FNET_GD = 128
PROJ_W = ATTN_W + 2 * KV_W + 2 * SGU_W + FNET_W
BLOCK = 128
WINDOW = 128
CHUNK = 128
ROPE_BASE = 10000.0
N_MOD = 6
PEER_HEADS = 8
PEER_NKEYS = 128
PEER_N = PEER_NKEYS * PEER_NKEYS
PEER_TOPK = 16
PEER_QDIM = 128
PEER_HALF = PEER_QDIM // 2
DN_ALPHA = (2 * DEPTH) ** 0.25
LN_EPS = 1e-5
NEG_INF = -1e30

F32 = jnp.float32
BF16 = jnp.bfloat16
LANES = 128
SUBLANES = 8
V7X_VMEM_BYTES = 64 * 2 ** 20
VMEM_LIMIT = V7X_VMEM_BYTES - 8 * 2 ** 20
ROW_TILE = 256
PEER_TOKEN_TILE = 512
PEER_EXPERT_TILE = SUBLANES * PEER_NKEYS
MOD_COL_TILE = 1024
MOD_ROWS = 16

_NT = (((1,), (1,)), ((), ()))


def _cparams(*sem):
    return pltpu.CompilerParams(dimension_semantics=sem, vmem_limit_bytes=VMEM_LIMIT)


def _ln(x):
    mu = jnp.mean(x, axis=-1, keepdims=True)
    xc = x - mu
    var = jnp.mean(xc * xc, axis=-1, keepdims=True)
    return xc * lax.rsqrt(var + LN_EPS)


def _gelu(x):
    return 0.5 * x * (1.0 + lax.erf(x * (0.5 ** 0.5)))


def _mod_kernel(c_ref, w_ref, b_ref, o_ref):
    c = c_ref[...]
    a = (c / (1.0 + jnp.exp(-c))).astype(BF16)
    o_ref[...] = jnp.dot(a, w_ref[...].astype(BF16), preferred_element_type=F32) + b_ref[...]


def _modulation(cvec, w_mod, b_mod):
    n = N_MOD * D_MODEL
    out = pl.pallas_call(
        _mod_kernel,
        grid=(DEPTH, n // MOD_COL_TILE),
        in_specs=[
            pl.BlockSpec((MOD_ROWS, D_MODEL), lambda l, j: (0, 0)),
            pl.BlockSpec((None, D_MODEL, MOD_COL_TILE), lambda l, j: (l, 0, j)),
            pl.BlockSpec((None, 1, MOD_COL_TILE), lambda l, j: (l, 0, j)),
        ],
        out_specs=pl.BlockSpec((None, MOD_ROWS, MOD_COL_TILE), lambda l, j: (l, 0, j)),
        out_shape=jax.ShapeDtypeStruct((DEPTH, MOD_ROWS, n), F32),
        compiler_params=_cparams("arbitrary", "arbitrary"),
        name="modulation",
    )(cvec, w_mod, b_mod.reshape(DEPTH, 1, n))
    return out.reshape(DEPTH, MOD_ROWS, N_MOD, D_MODEL)


_PROJ_SLICES = ((0, ATTN_W), (ATTN_W, KV_W), (ATTN_W + KV_W, KV_W),
                (ATTN_W + 2 * KV_W, SGU_W), (ATTN_W + 2 * KV_W + SGU_W, SGU_W),
                (ATTN_W + 2 * KV_W + 2 * SGU_W, FNET_W))


def _proj_kernel(x_ref, mod_ref, w_ref, *out_refs):
    h = _ln(x_ref[...]) * (1.0 + mod_ref[1:2, :]) + mod_ref[0:1, :]
    h = h.astype(BF16)
    for (start, width), o_ref in zip(_PROJ_SLICES, out_refs):
        o_ref[...] = jnp.dot(h, w_ref[:, start:start + width], preferred_element_type=F32)


def _in_projection(x, mod, mod_base, rows_per_mod, w_in):
    t = x.shape[0]
    tiles_per_mod = rows_per_mod // ROW_TILE
    row = lambda w: pl.BlockSpec((ROW_TILE, w), lambda i: (i, 0))
    return pl.pallas_call(
        _proj_kernel,
        grid=(t // ROW_TILE,),
        in_specs=[
            row(D_MODEL),
            pl.BlockSpec((None, N_MOD, D_MODEL), lambda i: (mod_base + i // tiles_per_mod, 0, 0)),
            pl.BlockSpec((D_MODEL, PROJ_W), lambda i: (0, 0)),
        ],
        out_specs=[row(w) for _, w in _PROJ_SLICES],
        out_shape=[jax.ShapeDtypeStruct((t, w), F32) for _, w in _PROJ_SLICES],
        compiler_params=_cparams("arbitrary"),
        name="in_projection",
    )(x, mod, w_in)


def _ctx_attn_kernel(q_ref, k_ref, v_ref, sink_ref, o_ref):
    k =k_ref[...].astype(BF16)
    v = v_ref[...].astype(BF16)
    scale = HEAD_DIM ** -0.5
    for g in range(GROUP):
        cols = slice(g * HEAD_DIM, (g + 1) * HEAD_DIM)
        q = q_ref[:, cols].astype(BF16)
        s = lax.dot_general(q, k, _NT, preferred_element_type=F32) * scale
        sk = sink_ref[g:g + 1, 0:1]
        m = jnp.maximum(jnp.max(s, axis=-1, keepdims=True), sk)
        p = jnp.exp(s - m)
        den = jnp.sum(p, axis=-1, keepdims=True) + jnp.exp(sk - m)
        o = jnp.dot(p.astype(BF16), v, preferred_element_type=F32)
        o_ref[:, cols] = o / den


def _context_attention(q, k, v, sink_b, batch, seq):
    t = q.shape[0]
    qspec = pl.BlockSpec((seq, GROUP * HEAD_DIM), lambda b, h: (b, h))
    kvspec = pl.BlockSpec((seq, HEAD_DIM), lambda b, h: (b, h))
    return pl.pallas_call(
        _ctx_attn_kernel,
        grid=(batch, N_KV),
        in_specs=[qspec, kvspec, kvspec, pl.BlockSpec((None, SUBLANES, LANES), lambda b, h: (h, 0, 0))],
        out_specs=qspec,
        out_shape=jax.ShapeDtypeStruct((t, ATTN_W), F32),
        compiler_params=_cparams("arbitrary", "arbitrary"),
        name="context_attention",
    )(q, k, v, sink_b)


def _lat_attn_kernel(q_ref, k_ref, v_ref, ck_ref, cv_ref, cos_ref, sin_ref, sink_ref, o_ref,
                     kp_ref, vp_ref, *, seq):
    scale = HEAD_DIM ** -0.5
    lane = lax.broadcasted_iota(jnp.int32, (1, HEAD_DIM), 1)
    first = (lane % 64) < 32

    def rope(x, cos, sin):
        sw = jnp.where(first, pltpu.roll(x, 96, 1), pltpu.roll(x, 32, 1))
        return x * cos + sw * sin

    pad = jnp.zeros((BLOCK, HEAD_DIM), BF16)
    kp_ref[0:BLOCK, :] = pad
    kp_ref[seq + BLOCK:seq + 2 * BLOCK, :] = pad
    vp_ref[0:BLOCK, :] = pad
    vp_ref[seq + BLOCK:seq + 2 * BLOCK, :] = pad
    kp_ref[BLOCK:seq + BLOCK, :] = rope(k_ref[...], cos_ref[...], sin_ref[...]).astype(BF16)
    vp_ref[BLOCK:seq + BLOCK, :] = v_ref[...].astype(BF16)
    ck = ck_ref[...].astype(BF16)
    cv = cv_ref[...].astype(BF16)

    def body(qb, carry):
        r0 = pl.multiple_of(qb * BLOCK, BLOCK)
        kb = kp_ref[pl.ds(r0, 3 * BLOCK), :]
        vb = vp_ref[pl.ds(r0, 3 * BLOCK), :]
        cosq = cos_ref[pl.ds(r0, BLOCK), :]
        sinq = sin_ref[pl.ds(r0, BLOCK), :]
        r = lax.broadcasted_iota(jnp.int32, (BLOCK, 3 * BLOCK), 0)
        j = lax.broadcasted_iota(jnp.int32, (BLOCK, 3 * BLOCK), 1)
        kpos = j + (r0 - BLOCK)
        valid = (j >= r) & (j <= r + 2 * WINDOW) & (kpos >= 0) & (kpos < seq)
        for g in range(GROUP):
            cols = slice(g * HEAD_DIM, (g + 1) * HEAD_DIM)
            q = rope(q_ref[pl.ds(r0, BLOCK), cols], cosq, sinq).astype(BF16)
            sl = lax.dot_general(q, kb, _NT, preferred_element_type=F32) * scale
            sl = jnp.where(valid, sl, NEG_INF)
            sc = lax.dot_general(q, ck, _NT, preferred_element_type=F32) * scale
            sk = sink_ref[g:g + 1, 0:1]
            m = jnp.maximum(jnp.maximum(jnp.max(sl, axis=-1, keepdims=True),
                                        jnp.max(sc, axis=-1, keepdims=True)), sk)
            p_l = jnp.exp(sl - m)
            p_c = jnp.exp(sc - m)
            den = (jnp.sum(p_l, axis=-1, keepdims=True) + jnp.sum(p_c, axis=-1, keepdims=True)
                   + jnp.exp(sk - m))
            o = (jnp.dot(p_l.astype(BF16), vb, preferred_element_type=F32)
                 + jnp.dot(p_c.astype(BF16), cv, preferred_element_type=F32))
            o_ref[pl.ds(r0, BLOCK), cols] = o / den
        return carry

    lax.fori_loop(0, seq // BLOCK, body, 0)


def _latent_attention(q, k, v, cache_k, cache_v, layer, cos_t, sin_t, sink_b, batch, seq):
    t = q.shape[0]
    past = cache_k.shape[2]
    qspec = pl.BlockSpec((seq, GROUP * HEAD_DIM), lambda b, h: (b, h))
    kvspec = pl.BlockSpec((seq, HEAD_DIM), lambda b, h: (b, h))
    cspec = pl.BlockSpec((None, None, past, HEAD_DIM), lambda b, h: (b, layer, 0, h))
    tspec = pl.BlockSpec((seq, HEAD_DIM), lambda b, h: (0, 0))
    return pl.pallas_call(
        functools.partial(_lat_attn_kernel, seq=seq),
        grid=(batch, N_KV),
        in_specs=[qspec, kvspec, kvspec, cspec, cspec, tspec, tspec,
                  pl.BlockSpec((None, SUBLANES, LANES), lambda b, h: (h, 0, 0))],
        out_specs=qspec,
        out_shape=jax.ShapeDtypeStruct((t, ATTN_W), F32),
        scratch_shapes=[pltpu.VMEM((seq + 2 * BLOCK, HEAD_DIM), BF16),
                        pltpu.VMEM((seq + 2 * BLOCK, HEAD_DIM), BF16)],
        compiler_params=_cparams("arbitrary", "arbitrary"),
        name="latent_attention",
    )(q, k, v, cache_k, cache_v, cos_t, sin_t, sink_b)


def _rope_tables(seq):
    pos = jnp.arange(seq)
    r = (pos // GRID_W).astype(F32)
    col = (pos % GRID_W).astype(F32)
    n = HEAD_DIM // 4
    inv = ROPE_BASE ** (-jnp.arange(n, dtype=F32) / n)
    ang_r = r[:, None] * inv
    ang_c = col[:, None] * inv
    cos_t = jnp.concatenate([jnp.cos(ang_r), jnp.cos(ang_r), jnp.cos(ang_c), jnp.cos(ang_c)], axis=-1)
    sin_t = jnp.concatenate([-jnp.sin(ang_r), jnp.sin(ang_r), -jnp.sin(ang_c), jnp.sin(ang_c)], axis=-1)
    return cos_t, sin_t


def _sgu_kernel(ug_ref, vg_ref, g_ref, ws_ref, bias_ref, o_ref):
    for c in range(ROW_TILE // CHUNK):
        rows = slice(c * CHUNK, (c + 1) * CHUNK)
        for g in range(SGU_GROUPS):
            cols = slice(g * LANES, (g + 1) * LANES)
            u = _gelu(ug_ref[rows, cols])
            vn = _ln(_gelu(vg_ref[rows, cols])) * g_ref[0:1, cols]
            mixed = jnp.dot(ws_ref[g], vn.astype(BF16), preferred_element_type=F32) + bias_ref[:, cols]
            o_ref[rows, cols] = u * mixed


def _spatial_gating(ug, vg, sgu_g, sgu_w, sgu_bias):
    t = ug.shape[0]
    row = pl.BlockSpec((ROW_TILE, SGU_W), lambda i: (i, 0))
    return pl.pallas_call(
        _sgu_kernel,
        grid=(t // ROW_TILE,),
        in_specs=[row, row,
                  pl.BlockSpec((1, SGU_W), lambda i: (0, 0)),
                  pl.BlockSpec((SGU_GROUPS, CHUNK, CHUNK), lambda i: (0, 0, 0)),
                  pl.BlockSpec((CHUNK, SGU_W), lambda i: (0, 0))],
        out_specs=row,
        out_shape=jax.ShapeDtypeStruct((t, SGU_W), F32),
        compiler_params=_cparams("arbitrary"),
        name="spatial_gating",
    )(ug, vg, sgu_g, sgu_w, sgu_bias)


def _fnet_kernel(f_ref, cc_ref, cs_ref, o_ref, z_ref, *, seq):
    @pl.when(pl.program_id(1) == 0)
    def _():
        cc = cc_ref[...]
        for g in range(FNET_GROUPS):
            cols = slice(g * FNET_GD, (g + 1) * FNET_GD)
            zz = jnp.dot(f_ref[:, cols].astype(BF16), cc, preferred_element_type=F32)
            z_ref[0:seq, cols] = zz[:, :FNET_GD].astype(BF16)
            z_ref[seq:2 * seq, cols] = zz[:, FNET_GD:].astype(BF16)

    y = jnp.dot(cs_ref[...], z_ref[...], preferred_element_type=F32)
    o_ref[...] = y * ((seq * FNET_GD) ** -0.5)


def _dft_tables(n):
    k = jnp.arange(n, dtype=jnp.int32)
    ang = ((k[:, None] * k[None, :]) % n).astype(F32) * (2.0 * jnp.pi / n)
    return jnp.cos(ang), jnp.sin(ang)


def _fourier_mix(f, batch, seq):
    t = f.shape[0]
    tr = min(seq, 512)
    c_c, s_c = _dft_tables(FNET_GD)
    c_s, s_s = _dft_tables(seq)
    cc = jnp.concatenate([c_c, s_c], axis=1).astype(BF16)
    cs = jnp.concatenate([c_s, -s_s], axis=1).astype(BF16)
    nj = seq // tr
    return pl.pallas_call(
        functools.partial(_fnet_kernel, seq=seq),
        grid=(batch, nj),
        in_specs=[pl.BlockSpec((seq, FNET_W), lambda b, j: (b, 0)),
                  pl.BlockSpec((FNET_GD, 2 * FNET_GD), lambda b, j: (0, 0)),
                  pl.BlockSpec((tr, 2 * seq), lambda b, j: (j, 0))],
        out_specs=pl.BlockSpec((tr, FNET_W), lambda b, j: (b * nj + j, 0)),
        out_shape=jax.ShapeDtypeStruct((t, FNET_W), F32),
        scratch_shapes=[pltpu.VMEM((2 * seq, FNET_W), BF16)],
        compiler_params=_cparams("arbitrary", "arbitrary"),
        name="fourier_mix",
    )(f, cc, cs)


def _mix_kernel(x_ref, a_ref, s_ref, f_ref, wo_ref, mod_ref, g1_ref, b1_ref, wq_ref,
                x1_ref, h2_ref, qp_ref):
    mix = jnp.dot(a_ref[...].astype(BF16), wo_ref[0:ATTN_W, :], preferred_element_type=F32)
    mix += jnp.dot(s_ref[...].astype(BF16), wo_ref[ATTN_W:ATTN_W + SGU_W, :], preferred_element_type=F32)
    mix += jnp.dot(f_ref[...].astype(BF16), wo_ref[ATTN_W + SGU_W:, :], preferred_element_type=F32)
    x1 = _ln(DN_ALPHA * x_ref[...] + mod_ref[2:3, :] * mix) * g1_ref[...] + b1_ref[...]
    x1_ref[...] = x1
    h2 = (_ln(x1) * (1.0 + mod_ref[4:5, :]) + mod_ref[3:4, :]).astype(BF16)
    h2_ref[...] = h2
    q = jnp.dot(h2, wq_ref[...], preferred_element_type=F32)
    for h in range(PEER_HEADS):
        qp_ref[h] = q[:, h * PEER_QDIM:(h + 1) * PEER_QDIM].astype(BF16)


def _out_projection(x, attn, sg, fn, w_out, mod, mod_base, rows_per_mod, ln_g, ln_b, wq):
    t = x.shape[0]
    tiles_per_mod = rows_per_mod // ROW_TILE
    row = lambda w: pl.BlockSpec((ROW_TILE, w), lambda i: (i, 0))
    full = lambda a, b: pl.BlockSpec((a, b), lambda i: (0, 0))
    return pl.pallas_call(
        _mix_kernel,
        grid=(t // ROW_TILE,),
        in_specs=[row(D_MODEL), row(ATTN_W), row(SGU_W), row(FNET_W),
                  full(D_MODEL, D_MODEL),
                  pl.BlockSpec((None, N_MOD, D_MODEL), lambda i: (mod_base + i // tiles_per_mod, 0, 0)),
                  full(1, D_MODEL), full(1, D_MODEL),
                  full(D_MODEL, PEER_HEADS * PEER_QDIM)],
        out_specs=[row(D_MODEL), row(D_MODEL),
                   pl.BlockSpec((PEER_HEADS, ROW_TILE, PEER_QDIM), lambda i: (0, i, 0))],
        out_shape=[jax.ShapeDtypeStruct((t, D_MODEL), F32),
                   jax.ShapeDtypeStruct((t, D_MODEL), BF16),
                   jax.ShapeDtypeStruct((PEER_HEADS, t, PEER_QDIM), BF16)],
        compiler_params=_cparams("arbitrary"),
        name="out_projection",
    )(x, attn, sg, fn, w_out, mod, ln_g, ln_b, wq)


def _top16_rows(s):
    n, w = s.shape
    row = lax.broadcasted_iota(jnp.int32, (n, w), 0).astype(F32)
    slot = lax.broadcasted_iota(jnp.int32, (PEER_TOPK, w), 0).astype(F32)

    def body(k, carry):
        cur, rank, vals = carry
        kf = k.astype(F32)
        m = jnp.max(cur, axis=0, keepdims=True)
        idx = jnp.min(jnp.where(cur == m, row, float(n)), axis=0, keepdims=True)
        hit = row == idx
        cur = jnp.where(hit, -jnp.inf, cur)
        rank = jnp.where(hit, kf, rank)
        vals = jnp.where(slot == kf, m, vals)
        return cur, rank, vals

    init = (s, jnp.full((n, w), float(PEER_TOPK), F32), jnp.zeros((PEER_TOPK, w), F32))
    _, rank, vals = lax.fori_loop(0, PEER_TOPK, body, init)
    return rank, vals


def _topk_kernel(q_ref, km_ref, r2_ref, e2_ref, n1_ref, e1_ref, *, tm):
    km = km_ref[...]
    for c in range(tm // LANES):
        tok = slice(c * LANES, (c + 1) * LANES)
        st = lax.dot_general(km, q_ref[tok, :], _NT, preferred_element_type=F32)
        s1 = st[0:PEER_NKEYS]
        s2 = st[PEER_NKEYS:2 * PEER_NKEYS]
        rank1, v1 = _top16_rows(s1)
        rank2, v2 = _top16_rows(s2)

        pieces = [v1[0:1] + v2]
        pos_pieces = [lax.broadcasted_iota(jnp.int32, (PEER_TOPK, LANES), 0)]
        for r1 in range(1, PEER_TOPK):
            pieces.append(v1[r1:r1 + 1] + v2[0:8])
            pos_pieces.append(lax.broadcasted_iota(jnp.int32, (8, LANES), 0) + r1 * PEER_TOPK)
        cand0 = jnp.concatenate(pieces, axis=0)
        pos = jnp.concatenate(pos_pieces, axis=0).astype(F32)
        slot = lax.broadcasted_iota(jnp.int32, (PEER_TOPK, LANES), 0).astype(F32)
        top = v1[0:1] + v2[0:1]

        def body(k, carry):
            cand, cnt, z = carry
            m = jnp.max(cand, axis=0, keepdims=True)
            p = jnp.min(jnp.where(cand == m, pos, 4096.0), axis=0, keepdims=True)
            cand = jnp.where(pos == p, -jnp.inf, cand)
            r1 = jnp.floor(p * (1.0 / PEER_TOPK))
            cnt = cnt + jnp.where(slot == r1, 1.0, 0.0)
            z = z + jnp.exp(m - top)
            return cand, cnt, z

        init = (cand0, jnp.zeros((PEER_TOPK, LANES), F32), jnp.zeros((1, LANES), F32))
        _, cnt, z = lax.fori_loop(0, PEER_TOPK, body, init)

        n1 = jnp.zeros((PEER_NKEYS, LANES), F32)
        for r1 in range(PEER_TOPK):
            n1 = jnp.where(rank1 == float(r1), cnt[r1:r1 + 1], n1)
        r2_ref[:, tok] = rank2
        e2_ref[:, tok] = jnp.exp(s2 - v2[0:1])
        n1_ref[:, tok] = n1
        e1_ref[:, tok] = jnp.exp(s1 - v1[0:1]) / z


def _peer_topk(qp, kmat):
    t = qp.shape[1]
    tm = ROW_TILE
    out_spec = pl.BlockSpec((PEER_NKEYS, tm), lambda i, h: (h, i))
    shape = jax.ShapeDtypeStruct((PEER_HEADS * PEER_NKEYS, t), F32)
    return pl.pallas_call(
        functools.partial(_topk_kernel, tm=tm),
        grid=(t // tm, PEER_HEADS),
        in_specs=[pl.BlockSpec((None, tm, PEER_QDIM), lambda i, h: (h, i, 0)),
                  pl.BlockSpec((None, 2 * PEER_NKEYS, PEER_QDIM), lambda i, h: (h, 0, 0))],
        out_specs=[out_spec] * 4,
        out_shape=[shape] * 4,
        compiler_params=_cparams("arbitrary", "arbitrary"),
        name="peer_topk",
    )(qp, kmat)


def _peer_kernel(h_ref, u_ref, vt_ref, r2_ref, e2_ref, n1_ref, e1_ref, o_ref, w_ref, *, tm, eb):
    e = pl.program_id(1)
    assert eb == SUBLANES * PEER_NKEYS

    @pl.when(e == 0)
    def _():
        o_ref[...] = jnp.zeros_like(o_ref)

    at = lax.dot_general(u_ref[...], h_ref[...], _NT, preferred_element_type=F32)
    for bi in range(SUBLANES):
        rows = slice(bi * PEER_NKEYS, (bi + 1) * PEER_NKEYS)
        for c in range(tm // LANES):
            tok = slice(c * LANES, (c + 1) * LANES)
            gate = jnp.zeros((PEER_NKEYS, LANES), F32)
            for h in range(PEER_HEADS):
                keys = slice(h * PEER_NKEYS, (h + 1) * PEER_NKEYS)
                grp = pl.ds(pl.multiple_of(h * PEER_NKEYS + e * SUBLANES, SUBLANES), SUBLANES)
                n1 = n1_ref[grp, tok][bi:bi + 1, :]
                e1 = e1_ref[grp, tok][bi:bi + 1, :]
                gate += jnp.where(r2_ref[keys, tok] < n1, e2_ref[keys, tok], 0.0) * e1
            w_ref[rows, tok] = (_gelu(at[rows, tok]) * gate).astype(BF16)
    o_ref[...] += jnp.dot(vt_ref[...], w_ref[...], preferred_element_type=F32)


def _peer_apply(h2, u, vt, tables):
    t = h2.shape[0]
    tm = min(PEER_TOKEN_TILE, t)
    eb = PEER_EXPERT_TILE
    tspec = pl.BlockSpec((PEER_HEADS * PEER_NKEYS, tm), lambda i, e: (0, i))
    return pl.pallas_call(
        functools.partial(_peer_kernel, tm=tm, eb=eb),
        grid=(t // tm, PEER_N // eb),
        in_specs=[pl.BlockSpec((tm, D_MODEL), lambda i, e: (i, 0)),
                  pl.BlockSpec((eb, D_MODEL), lambda i, e: (e, 0)),
                  pl.BlockSpec((D_MODEL, eb), lambda i, e: (0, e)),
                  tspec, tspec, tspec, tspec],
        out_specs=pl.BlockSpec((D_MODEL, tm), lambda i, e: (0, i)),
        out_shape=jax.ShapeDtypeStruct((D_MODEL, t), F32),
        scratch_shapes=[pltpu.VMEM((eb, tm), BF16)],
        compiler_params=_cparams("arbitrary", "arbitrary"),
        name="peer_apply",
    )(h2, u, vt, *tables)


def _final_kernel(x_ref, pt_ref, mod_ref, g_ref, b_ref, o_ref):
    y = DN_ALPHA * x_ref[...] + mod_ref[5:6, :] * pt_ref[...].T
    o_ref[...] = _ln(y) * g_ref[...] + b_ref[...]


def _finalize(x1, peer_t, mod, mod_base, rows_per_mod, ln_g, ln_b):
    t = x1.shape[0]
    tiles_per_mod = rows_per_mod // ROW_TILE
    row = pl.BlockSpec((ROW_TILE, D_MODEL), lambda i: (i, 0))
    vec = pl.BlockSpec((1, D_MODEL), lambda i: (0, 0))
    return pl.pallas_call(
        _final_kernel,
        grid=(t // ROW_TILE,),
        in_specs=[row,
                  pl.BlockSpec((D_MODEL, ROW_TILE), lambda i: (0, i)),
                  pl.BlockSpec((None, N_MOD, D_MODEL), lambda i: (mod_base + i // tiles_per_mod, 0, 0)),
                  vec, vec],
        out_specs=row,
        out_shape=jax.ShapeDtypeStruct((t, D_MODEL), F32),
        compiler_params=_cparams("arbitrary"),
        name="finalize",
    )(x1, peer_t, mod, ln_g, ln_b)


def _prep_layer(l, w_in, attn_sink, sgu_g, sgu_w, sgu_b, w_out, ln1_g, ln1_b,
                peer_wq, peer_keys, peer_u, peer_v, ln2_g, ln2_b):
    keys = peer_keys[l]
    zeros = jnp.zeros_like(keys[:, 0])
    kmat = jnp.concatenate([jnp.concatenate([keys[:, 0], zeros], axis=-1),
                            jnp.concatenate([zeros, keys[:, 1]], axis=-1)], axis=1)
    return dict(
        w_in=w_in[l].astype(BF16),
        sink_b=jnp.broadcast_to(
            jnp.pad(attn_sink[l].reshape(N_KV, GROUP), ((0, 0), (0, SUBLANES - GROUP)))[:, :, None],
            (N_KV, SUBLANES, LANES)),
        sgu_g=sgu_g[l].reshape(1, SGU_W),
        sgu_w=sgu_w[l].astype(BF16),
        sgu_bias=jnp.repeat(sgu_b[l].T, LANES, axis=1),
        w_out=w_out[l].astype(BF16),
        ln1_g=ln1_g[l].reshape(1, D_MODEL), ln1_b=ln1_b[l].reshape(1, D_MODEL),
        wq=peer_wq[l].astype(BF16),
        kmat=kmat.astype(BF16),
        u=peer_u[l].astype(BF16),
        vt=peer_v[l].T.astype(BF16),
        ln2_g=ln2_g[l].reshape(1, D_MODEL), ln2_b=ln2_b[l].reshape(1, D_MODEL),
    )


def _layer(x, batch, seq, mod, mod_base, rows_per_mod, lw, ctx):
    q, k, v, ug, vg, f = _in_projection(x, mod, mod_base, rows_per_mod, lw["w_in"])
    if ctx is None:
        attn = _context_attention(q, k, v, lw["sink_b"], batch, seq)
    else:
        cache_k, cache_v, layer, cos_t, sin_t = ctx
        attn = _latent_attention(q, k, v, cache_k, cache_v, layer, cos_t, sin_t, lw["sink_b"], batch, seq)
    sg = _spatial_gating(ug, vg, lw["sgu_g"], lw["sgu_w"], lw["sgu_bias"])
    fn = _fourier_mix(f, batch, seq)
    x1, h2, qp = _out_projection(x, attn, sg, fn, lw["w_out"], mod, mod_base, rows_per_mod,
                                 lw["ln1_g"], lw["ln1_b"], lw["wq"])
    tables = _peer_topk(qp, lw["kmat"])
    peer_t = _peer_apply(h2, lw["u"], lw["vt"], tables)
    x2 = _finalize(x1, peer_t, mod, mod_base, rows_per_mod, lw["ln2_g"], lw["ln2_b"])
    return x2, k, v


def kernel(x_prompt, x_sample, cache_k, cache_v, c, c_ctx, w_mod, b_mod, w_in, attn_sink, sgu_g, sgu_w, sgu_b, w_out, ln1_g, ln1_b, peer_wq, peer_keys, peer_u, peer_v, ln2_g, ln2_b):
    batch, seq, _ = x_prompt.shape
    dec_batch, dec_seq, _ = x_sample.shape
    past = cache_k.shape[2]
    assert 1 + dec_batch <= MOD_ROWS

    cvec = jnp.concatenate([c_ctx[None, :], c, jnp.zeros((MOD_ROWS - 1 - dec_batch, D_MODEL), F32)], axis=0)
    mod = _modulation(cvec, w_mod, b_mod)
    cos_t, sin_t = _rope_tables(dec_seq)
    ck = cache_k.reshape(dec_batch, DEPTH, past, KV_W)
    cv = cache_v.reshape(dec_batch, DEPTH, past, KV_W)

    xp = x_prompt.reshape(batch * seq, D_MODEL)
    xs = x_sample.reshape(dec_batch * dec_seq, D_MODEL)
    new_k, new_v = [], []
    for l in range(DEPTH):
        lw = _prep_layer(l, w_in, attn_sink, sgu_g, sgu_w, sgu_b, w_out, ln1_g, ln1_b,
                         peer_wq, peer_keys, peer_u, peer_v, ln2_g, ln2_b)
        xp, kl, vl = _layer(xp, batch, seq, mod[l], 0, batch * seq, lw, None)
        new_k.append(kl.reshape(batch, seq, N_KV, HEAD_DIM))
        new_v.append(vl.reshape(batch, seq, N_KV, HEAD_DIM))
        xs, _, _ = _layer(xs, dec_batch, dec_seq, mod[l], 1, dec_seq, lw, (ck, cv, l, cos_t, sin_t))
    return (xp.reshape(batch, seq, D_MODEL), xs.reshape(dec_batch, dec_seq, D_MODEL),
            jnp.stack(new_k, axis=1), jnp.stack(new_v, axis=1))
```

```python
import functools

import jax
import jax.numpy as jnp
from jax import lax
from jax.experimental import pallas as pl
from jax.experimental.pallas import tpu as pltpu

D_MODEL = 2048
DEPTH = 2
GRID_W = 64
HEAD_DIM = 128
N_HEADS = 8
N_KV = 2
GROUP = N_HEADS // N_KV
ATTN_W = N_HEADS * HEAD_DIM
KV_W = N_KV * HEAD_DIM
SGU_GROUPS = 4
SGU_W = 512
FNET_GROUPS = 4
FNET_W = 512
FNET_GD = 128
PROJ_W = ATTN_W + 2 * KV_W + 2 * SGU_W + FNET_W
BLOCK = 128
WINDOW = 128
CHUNK = 128
ROPE_BASE = 10000.0
N_MOD = 6
PEER_HEADS = 8
PEER_NKEYS = 128
PEER_N = PEER_NKEYS * PEER_NKEYS
PEER_TOPK = 16
PEER_QDIM = 128
PEER_HALF = PEER_QDIM // 2
DN_ALPHA = (2 * DEPTH) ** 0.25
LN_EPS = 1e-5
NEG_INF = -1e30

F32 = jnp.float32
BF16 = jnp.bfloat16
LANES = 128
SUBLANES = 8
V7X_VMEM_BYTES = 64 * 2 ** 20
VMEM_LIMIT = V7X_VMEM_BYTES - 8 * 2 ** 20
ROW_TILE = 256
PEER_TOKEN_TILE = 512
PEER_EXPERT_SUBTILE = 256
PEER_EXPERT_TILE = SUBLANES * PEER_NKEYS
MOD_COL_TILE = 1024
MOD_ROWS = 16

_NT = (((1,), (1,)), ((), ()))


def _cparams(*sem):
    return pltpu.CompilerParams(dimension_semantics=sem, vmem_limit_bytes=VMEM_LIMIT)


def _ln(x):
    mu = jnp.mean(x, axis=-1, keepdims=True)
    xc = x - mu
    var = jnp.mean(xc * xc, axis=-1, keepdims=True)
    return xc * lax.rsqrt(var + LN_EPS)


def _gelu(x):
    return 0.5 * x * (1.0 + lax.erf(x * (0.5 ** 0.5)))


def _mod_kernel(c_ref, w_ref, b_ref, o_ref):
    c = c_ref[...]
    a = (c / (1.0 + jnp.exp(-c))).astype(BF16)
    o_ref[...] = jnp.dot(a, w_ref[...].astype(BF16), preferred_element_type=F32) + b_ref[...]


def _modulation(cvec, w_mod, b_mod):
    n = N_MOD * D_MODEL
    out = pl.pallas_call(
        _mod_kernel,
        grid=(DEPTH, n // MOD_COL_TILE),
        in_specs=[
            pl.BlockSpec((MOD_ROWS, D_MODEL), lambda l, j: (0, 0)),
            pl.BlockSpec((None, D_MODEL, MOD_COL_TILE), lambda l, j: (l, 0, j)),
            pl.BlockSpec((None, 1, MOD_COL_TILE), lambda l, j: (l, 0, j)),
        ],
        out_specs=pl.BlockSpec((None, MOD_ROWS, MOD_COL_TILE), lambda l, j: (l, 0, j)),
        out_shape=jax.ShapeDtypeStruct((DEPTH, MOD_ROWS, n), F32),
        compiler_params=_cparams("arbitrary", "arbitrary"),
        name="modulation",
    )(cvec, w_mod, b_mod.reshape(DEPTH, 1, n))
    return out.reshape(DEPTH, MOD_ROWS, N_MOD, D_MODEL)


_PROJ_SLICES = ((0, ATTN_W), (ATTN_W, KV_W), (ATTN_W + KV_W, KV_W),
                (ATTN_W + 2 * KV_W, SGU_W), (ATTN_W + 2 * KV_W + SGU_W, SGU_W),
                (ATTN_W + 2 * KV_W + 2 * SGU_W, FNET_W))


def _proj_kernel(x_ref, mod_ref, w_ref, *out_refs):
    h = _ln(x_ref[...]) * (1.0 + mod_ref[1:2, :]) + mod_ref[0:1, :]
    h = h.astype(BF16)
    for (start, width), o_ref in zip(_PROJ_SLICES, out_refs):
        o_ref[...] = jnp.dot(h, w_ref[:, start:start + width], preferred_element_type=F32)


def _in_projection(x, mod, mod_base, rows_per_mod, w_in):
    t = x.shape[0]
    tiles_per_mod = rows_per_mod // ROW_TILE
    row = lambda w: pl.BlockSpec((ROW_TILE, w), lambda i: (i, 0))
    return pl.pallas_call(
        _proj_kernel,
        grid=(t // ROW_TILE,),
        in_specs=[
            row(D_MODEL),
            pl.BlockSpec((None, N_MOD, D_MODEL), lambda i: (mod_base + i // tiles_per_mod, 0, 0)),
            pl.BlockSpec((D_MODEL, PROJ_W), lambda i: (0, 0)),
        ],
        out_specs=[row(w) for _, w in _PROJ_SLICES],
        out_shape=[jax.ShapeDtypeStruct((t, w), F32) for _, w in _PROJ_SLICES],
        compiler_params=_cparams("arbitrary"),
        name="in_projection",
    )(x, mod, w_in)


def _ctx_attn_kernel(q_ref, k_ref, v_ref, sink_ref, o_ref):
    k =k_ref[...].astype(BF16)
    v = v_ref[...].astype(BF16)
    scale = HEAD_DIM ** -0.5
    for g in range(GROUP):
        cols = slice(g * HEAD_DIM, (g + 1) * HEAD_DIM)
        q = q_ref[:, cols].astype(BF16)
        s = lax.dot_general(q, k, _NT, preferred_element_type=F32) * scale
        sk = sink_ref[g:g + 1, 0:1]
        m = jnp.maximum(jnp.max(s, axis=-1, keepdims=True), sk)
        p = jnp.exp(s - m)
        den = jnp.sum(p, axis=-1, keepdims=True) + jnp.exp(sk - m)
        o = jnp.dot(p.astype(BF16), v, preferred_element_type=F32)
        o_ref[:, cols] = o / den


def _context_attention(q, k, v, sink_b, batch, seq):
    t = q.shape[0]
    qspec = pl.BlockSpec((seq, GROUP * HEAD_DIM), lambda b, h: (b, h))
    kvspec = pl.BlockSpec((seq, HEAD_DIM), lambda b, h: (b, h))
    return pl.pallas_call(
        _ctx_attn_kernel,
        grid=(batch, N_KV),
        in_specs=[qspec, kvspec, kvspec, pl.BlockSpec((None, SUBLANES, LANES), lambda b, h: (h, 0, 0))],
        out_specs=qspec,
        out_shape=jax.ShapeDtypeStruct((t, ATTN_W), F32),
        compiler_params=_cparams("arbitrary", "arbitrary"),
        name="context_attention",
    )(q, k, v, sink_b)


def _lat_attn_kernel(q_ref, k_ref, v_ref, ck_ref, cv_ref, cos_ref, sin_ref, sink_ref, o_ref,
                     kp_ref, vp_ref, *, seq):
    scale = HEAD_DIM ** -0.5
    lane = lax.broadcasted_iota(jnp.int32, (1, HEAD_DIM), 1)
    first = (lane % 64) < 32

    def rope(x, cos, sin):
        sw = jnp.where(first, pltpu.roll(x, 96, 1), pltpu.roll(x, 32, 1))
        return x * cos + sw * sin

    pad = jnp.zeros((BLOCK, HEAD_DIM), BF16)
    kp_ref[0:BLOCK, :] = pad
    kp_ref[seq + BLOCK:seq + 2 * BLOCK, :] = pad
    vp_ref[0:BLOCK, :] = pad
    vp_ref[seq + BLOCK:seq + 2 * BLOCK, :] = pad
    kp_ref[BLOCK:seq + BLOCK, :] = rope(k_ref[...], cos_ref[...], sin_ref[...]).astype(BF16)
    vp_ref[BLOCK:seq + BLOCK, :] = v_ref[...].astype(BF16)
    ck = ck_ref[...].astype(BF16)
    cv = cv_ref[...].astype(BF16)

    def body(qb, carry):
        r0 = pl.multiple_of(qb * BLOCK, BLOCK)
        kb = kp_ref[pl.ds(r0, 3 * BLOCK), :]
        vb = vp_ref[pl.ds(r0, 3 * BLOCK), :]
        cosq = cos_ref[pl.ds(r0, BLOCK), :]
        sinq = sin_ref[pl.ds(r0, BLOCK), :]
        r = lax.broadcasted_iota(jnp.int32, (BLOCK, 3 * BLOCK), 0)
        j = lax.broadcasted_iota(jnp.int32, (BLOCK, 3 * BLOCK), 1)
        kpos = j + (r0 - BLOCK)
        valid = (j >= r) & (j <= r + 2 * WINDOW) & (kpos >= 0) & (kpos < seq)
        for g in range(GROUP):
            cols = slice(g * HEAD_DIM, (g + 1) * HEAD_DIM)
            q = rope(q_ref[pl.ds(r0, BLOCK), cols], cosq, sinq).astype(BF16)
            sl = lax.dot_general(q, kb, _NT, preferred_element_type=F32) * scale
            sl = jnp.where(valid, sl, NEG_INF)
            sc = lax.dot_general(q, ck, _NT, preferred_element_type=F32) * scale
            sk = sink_ref[g:g + 1, 0:1]
            m = jnp.maximum(jnp.maximum(jnp.max(sl, axis=-1, keepdims=True),
                                        jnp.max(sc, axis=-1, keepdims=True)), sk)
            p_l = jnp.exp(sl - m)
            p_c = jnp.exp(sc - m)
            den = (jnp.sum(p_l, axis=-1, keepdims=True) + jnp.sum(p_c, axis=-1, keepdims=True)
                   + jnp.exp(sk - m))
            o = (jnp.dot(p_l.astype(BF16), vb, preferred_element_type=F32)
                 + jnp.dot(p_c.astype(BF16), cv, preferred_element_type=F32))
            o_ref[pl.ds(r0, BLOCK), cols] = o / den
        return carry

    lax.fori_loop(0, seq // BLOCK, body, 0)


def _latent_attention(q, k, v, cache_k, cache_v, layer, cos_t, sin_t, sink_b, batch, seq):
    t = q.shape[0]
    past = cache_k.shape[2]
    qspec = pl.BlockSpec((seq, GROUP * HEAD_DIM), lambda b, h: (b, h))
    kvspec = pl.BlockSpec((seq, HEAD_DIM), lambda b, h: (b, h))
    cspec = pl.BlockSpec((None, None, past, HEAD_DIM), lambda b, h: (b, layer, 0, h))
    tspec = pl.BlockSpec((seq, HEAD_DIM), lambda b, h: (0, 0))
    return pl.pallas_call(
        functools.partial(_lat_attn_kernel, seq=seq),
        grid=(batch, N_KV),
        in_specs=[qspec, kvspec, kvspec, cspec, cspec, tspec, tspec,
                  pl.BlockSpec((None, SUBLANES, LANES), lambda b, h: (h, 0, 0))],
        out_specs=qspec,
        out_shape=jax.ShapeDtypeStruct((t, ATTN_W), F32),
        scratch_shapes=[pltpu.VMEM((seq + 2 * BLOCK, HEAD_DIM), BF16),
                        pltpu.VMEM((seq + 2 * BLOCK, HEAD_DIM), BF16)],
        compiler_params=_cparams("arbitrary", "arbitrary"),
        name="latent_attention",
    )(q, k, v, cache_k, cache_v, cos_t, sin_t, sink_b)


def _rope_tables(seq):
    pos = jnp.arange(seq)
    r = (pos // GRID_W).astype(F32)
    col = (pos % GRID_W).astype(F32)
    n = HEAD_DIM // 4
    inv = ROPE_BASE ** (-jnp.arange(n, dtype=F32) / n)
    ang_r = r[:, None] * inv
    ang_c = col[:, None] * inv
    cos_t = jnp.concatenate([jnp.cos(ang_r), jnp.cos(ang_r), jnp.cos(ang_c), jnp.cos(ang_c)], axis=-1)
    sin_t = jnp.concatenate([-jnp.sin(ang_r), jnp.sin(ang_r), -jnp.sin(ang_c), jnp.sin(ang_c)], axis=-1)
    return cos_t, sin_t


def _sgu_kernel(ug_ref, vg_ref, g_ref, ws_ref, bias_ref, o_ref):
    for c in range(ROW_TILE // CHUNK):
        rows = slice(c * CHUNK, (c + 1) * CHUNK)
        for g in range(SGU_GROUPS):
            cols = slice(g * LANES, (g + 1) * LANES)
            u = _gelu(ug_ref[rows, cols])
            vn = _ln(_gelu(vg_ref[rows, cols])) * g_ref[0:1, cols]
            mixed = jnp.dot(ws_ref[g], vn.astype(BF16), preferred_element_type=F32) + bias_ref[:, cols]
            o_ref[rows, cols] = u * mixed


def _spatial_gating(ug, vg, sgu_g, sgu_w, sgu_bias):
    t = ug.shape[0]
    row = pl.BlockSpec((ROW_TILE, SGU_W), lambda i: (i, 0))
    return pl.pallas_call(
        _sgu_kernel,
        grid=(t // ROW_TILE,),
        in_specs=[row, row,
                  pl.BlockSpec((1, SGU_W), lambda i: (0, 0)),
                  pl.BlockSpec((SGU_GROUPS, CHUNK, CHUNK), lambda i: (0, 0, 0)),
                  pl.BlockSpec((CHUNK, SGU_W), lambda i: (0, 0))],
        out_specs=row,
        out_shape=jax.ShapeDtypeStruct((t, SGU_W), F32),
        compiler_params=_cparams("arbitrary"),
        name="spatial_gating",
    )(ug, vg, sgu_g, sgu_w, sgu_bias)


def _fnet_kernel(f_ref, cc_ref, cs_ref, o_ref, z_ref, *, seq):
    @pl.when(pl.program_id(1) == 0)
    def _():
        cc = cc_ref[...]
        for g in range(FNET_GROUPS):
            cols = slice(g * FNET_GD, (g + 1) * FNET_GD)
            zz = jnp.dot(f_ref[:, cols].astype(BF16), cc, preferred_element_type=F32)
            z_ref[0:seq, cols] = zz[:, :FNET_GD].astype(BF16)
            z_ref[seq:2 * seq, cols] = zz[:, FNET_GD:].astype(BF16)

    y = jnp.dot(cs_ref[...], z_ref[...], preferred_element_type=F32)
    o_ref[...] = y * ((seq * FNET_GD) ** -0.5)


def _dft_tables(n):
    k = jnp.arange(n, dtype=jnp.int32)
    ang = ((k[:, None] * k[None, :]) % n).astype(F32) * (2.0 * jnp.pi / n)
    return jnp.cos(ang), jnp.sin(ang)


def _fourier_mix(f, batch, seq):
    t = f.shape[0]
    tr = min(seq, 512)
    c_c, s_c = _dft_tables(FNET_GD)
    c_s, s_s = _dft_tables(seq)
    cc = jnp.concatenate([c_c, s_c], axis=1).astype(BF16)
    cs = jnp.concatenate([c_s, -s_s], axis=1).astype(BF16)
    nj = seq // tr
    return pl.pallas_call(
        functools.partial(_fnet_kernel, seq=seq),
        grid=(batch, nj),
        in_specs=[pl.BlockSpec((seq, FNET_W), lambda b, j: (b, 0)),
                  pl.BlockSpec((FNET_GD, 2 * FNET_GD), lambda b, j: (0, 0)),
                  pl.BlockSpec((tr, 2 * seq), lambda b, j: (j, 0))],
        out_specs=pl.BlockSpec((tr, FNET_W), lambda b, j: (b * nj + j, 0)),
        out_shape=jax.ShapeDtypeStruct((t, FNET_W), F32),
        scratch_shapes=[pltpu.VMEM((2 * seq, FNET_W), BF16)],
        compiler_params=_cparams("arbitrary", "arbitrary"),
        name="fourier_mix",
    )(f, cc, cs)


def _mix_kernel(x_ref, a_ref, s_ref, f_ref, wo_ref, mod_ref, g1_ref, b1_ref, wq_ref,
                x1_ref, h2_ref, qp_ref):
    mix = jnp.dot(a_ref[...].astype(BF16), wo_ref[0:ATTN_W, :], preferred_element_type=F32)
    mix += jnp.dot(s_ref[...].astype(BF16), wo_ref[ATTN_W:ATTN_W + SGU_W, :], preferred_element_type=F32)
    mix += jnp.dot(f_ref[...].astype(BF16), wo_ref[ATTN_W + SGU_W:, :], preferred_element_type=F32)
    x1 = _ln(DN_ALPHA * x_ref[...] + mod_ref[2:3, :] * mix) * g1_ref[...] + b1_ref[...]
    x1_ref[...] = x1
    h2 = (_ln(x1) * (1.0 + mod_ref[4:5, :]) + mod_ref[3:4, :]).astype(BF16)
    h2_ref[...] = h2
    q = jnp.dot(h2, wq_ref[...], preferred_element_type=F32)
    for h in range(PEER_HEADS):
        qp_ref[h] = q[:, h * PEER_QDIM:(h + 1) * PEER_QDIM].astype(BF16)


def _out_projection(x, attn, sg, fn, w_out, mod, mod_base, rows_per_mod, ln_g, ln_b, wq):
    t = x.shape[0]
    tiles_per_mod = rows_per_mod // ROW_TILE
    row = lambda w: pl.BlockSpec((ROW_TILE, w), lambda i: (i, 0))
    full = lambda a, b: pl.BlockSpec((a, b), lambda i: (0, 0))
    return pl.pallas_call(
        _mix_kernel,
        grid=(t // ROW_TILE,),
        in_specs=[row(D_MODEL), row(ATTN_W), row(SGU_W), row(FNET_W),
                  full(D_MODEL, D_MODEL),
                  pl.BlockSpec((None, N_MOD, D_MODEL), lambda i: (mod_base + i // tiles_per_mod, 0, 0)),
                  full(1, D_MODEL), full(1, D_MODEL),
                  full(D_MODEL, PEER_HEADS * PEER_QDIM)],
        out_specs=[row(D_MODEL), row(D_MODEL),
                   pl.BlockSpec((PEER_HEADS, ROW_TILE, PEER_QDIM), lambda i: (0, i, 0))],
        out_shape=[jax.ShapeDtypeStruct((t, D_MODEL), F32),
                   jax.ShapeDtypeStruct((t, D_MODEL), BF16),
                   jax.ShapeDtypeStruct((PEER_HEADS, t, PEER_QDIM), BF16)],
        compiler_params=_cparams("arbitrary"),
        name="out_projection",
    )(x, attn, sg, fn, w_out, mod, ln_g, ln_b, wq)


def _top16_rows(scores, exact_ties):
    n, w = scores[0].shape
    row = lax.broadcasted_iota(jnp.int32, (n, w), 0).astype(F32)
    slot = lax.broadcasted_iota(jnp.int32, (PEER_TOPK, w), 0).astype(F32)

    def body(k, carry):
        kf = k.astype(F32)
        out = []
        for cur, rank, vals in carry:
            m = jnp.max(cur, axis=0, keepdims=True)
            hit = cur == m
            if exact_ties:
                hit = row == jnp.min(jnp.where(hit, row, float(n)), axis=0, keepdims=True)
            out.append((jnp.where(hit, -jnp.inf, cur), jnp.where(hit, kf, rank),
                        jnp.where(slot == kf, m, vals)))
        return tuple(out)

    init = tuple((s, jnp.full((n, w), float(PEER_TOPK), F32), jnp.zeros((PEER_TOPK, w), F32))
                 for s in scores)
    res = lax.fori_loop(0, PEER_TOPK, body, init)
    return [(rank, vals) for _, rank, vals in res]


def _select_pairs(v1, v2, exact_ties):
    half = PEER_TOPK // 2
    pos_pieces = [lax.broadcasted_iota(jnp.int32, (PEER_TOPK, LANES), 0)]
    for r1 in range(1, PEER_TOPK):
        pos_pieces.append(lax.broadcasted_iota(jnp.int32, (half, LANES), 0) + r1 * PEER_TOPK)
    pos = jnp.concatenate(pos_pieces, axis=0).astype(F32)
    slot = lax.broadcasted_iota(jnp.int32, (PEER_TOPK, LANES), 0).astype(F32)
    cands, tops0 = [], []
    for a, b in zip(v1, v2):
        pieces = [a[0:1] + b] + [a[r1:r1 + 1] + b[0:half] for r1 in range(1, PEER_TOPK)]
        cands.append(jnp.concatenate(pieces, axis=0))
        tops0.append(a[0:1] + b[0:1])

    def body(k, carry):
        out = []
        for (cand, cnt, z), top in zip(carry, tops0):
            m = jnp.max(cand, axis=0, keepdims=True)
            hit = cand == m
            if exact_ties:
                p = jnp.min(jnp.where(hit, pos, 4096.0), axis=0, keepdims=True)
                hit = pos == p
                cnt = cnt + jnp.where(slot == jnp.floor(p * (1.0 / PEER_TOPK)), 1.0, 0.0)
            out.append((jnp.where(hit, -jnp.inf, cand), cnt, z + jnp.exp(m - top)))
        return tuple(out)

    init = tuple((cand, jnp.zeros((PEER_TOPK, LANES), F32), jnp.zeros((1, LANES), F32))
                 for cand in cands)
    res = lax.fori_loop(0, PEER_TOPK, body, init)
    out = []
    for cand, cnt, z in res:
        if exact_ties:
            rows = [cnt[r1:r1 + 1] for r1 in range(PEER_TOPK)]
        else:
            taken = jnp.where(cand == -jnp.inf, 1.0, 0.0)
            rows = [jnp.sum(taken[0:PEER_TOPK], axis=0, keepdims=True)]
            for r1 in range(1, PEER_TOPK):
                lo = PEER_TOPK + (r1 - 1) * half
                rows.append(jnp.sum(taken[lo:lo + half], axis=0, keepdims=True))
        out.append((rows, z))
    return out


def _topk_tables(s1, s2, exact_ties):
    nc = len(s1)
    tops = _top16_rows(s1 + s2, exact_ties)
    v1 = [tops[c][1] for c in range(nc)]
    v2 = [tops[nc + c][1] for c in range(nc)]
    pairs = _select_pairs(v1, v2, exact_ties)
    tables, bad = [], jnp.zeros((1, LANES), F32)
    for c in range(nc):
        rank1, rank2 = tops[c][0], tops[nc + c][0]
        cnt_rows, z = pairs[c]
        n1 = jnp.zeros((PEER_NKEYS, LANES), F32)
        for r1 in range(PEER_TOPK):
            n1 = jnp.where(rank1 == float(r1), cnt_rows[r1], n1)
        tables.append((rank2, jnp.exp(s2[c] - v2[c][0:1]), n1, jnp.exp(s1[c] - v1[c][0:1]) / z))
        if not exact_ties:
            k = float(PEER_TOPK)
            for rank in (rank1, rank2):
                ranked = jnp.sum(jnp.where(rank < k, 1.0, 0.0), axis=0, keepdims=True)
                bad = bad + jnp.abs(ranked - k)
            bad = bad + jnp.abs(sum(cnt_rows) - k)
    return tables, bad


def _topk_kernel(q_ref, km_ref, r2_ref, e2_ref, n1_ref, e1_ref, *, tm):
    km = km_ref[...]
    toks = [slice(c * LANES, (c + 1) * LANES) for c in range(tm // LANES)]
    sts = [lax.dot_general(km, q_ref[tok, :], _NT, preferred_element_type=F32) for tok in toks]
    s1 = [st[0:PEER_NKEYS] for st in sts]
    s2 = [st[PEER_NKEYS:2 * PEER_NKEYS] for st in sts]

    def store(tables):
        for tok, (rank2, e2, n1, e1) in zip(toks, tables):
            r2_ref[:, tok] = rank2.astype(r2_ref.dtype)
            e2_ref[:, tok] = e2.astype(e2_ref.dtype)
            n1_ref[:, tok] = n1
            e1_ref[:, tok] = e1

    tables, bad = _topk_tables(s1, s2, exact_ties=False)
    store(tables)

    @pl.when(jnp.max(bad) > 0.0)
    def _():
        store(_topk_tables(s1, s2, exact_ties=True)[0])


def _peer_topk(qp, kmat):
    t = qp.shape[1]
    tm = ROW_TILE
    out_spec = pl.BlockSpec((PEER_NKEYS, tm), lambda i, h: (h, i))
    shape = lambda dt: jax.ShapeDtypeStruct((PEER_HEADS * PEER_NKEYS, t), dt)
    return pl.pallas_call(
        functools.partial(_topk_kernel, tm=tm),
        grid=(t // tm, PEER_HEADS),
        in_specs=[pl.BlockSpec((None, tm, PEER_QDIM), lambda i, h: (h, i, 0)),
                  pl.BlockSpec((None, 2 * PEER_NKEYS, PEER_QDIM), lambda i, h: (h, 0, 0))],
        out_specs=[out_spec] * 4,
        out_shape=[shape(BF16), shape(BF16), shape(F32), shape(F32)],
        compiler_params=_cparams("arbitrary", "arbitrary"),
        name="peer_topk",
    )(qp, kmat)


def _peer_kernel(h_ref, u_ref, vt_ref, r2_ref, e2_ref, n1_ref, e1_ref, o_ref, w_ref, *, tm, n_steps):
    e = pl.program_id(1)
    pack = 2 * SUBLANES
    groups = PEER_NKEYS // pack

    def gated_activations(slot):
        for sb in range(PEER_EXPERT_TILE // PEER_EXPERT_SUBTILE):
            urows = slice(sb * PEER_EXPERT_SUBTILE, (sb + 1) * PEER_EXPERT_SUBTILE)
            at = lax.dot_general(u_ref[urows, :], h_ref[...], _NT, preferred_element_type=F32)
            for bs in range(PEER_EXPERT_SUBTILE // PEER_NKEYS):
                bi = sb * (PEER_EXPERT_SUBTILE // PEER_NKEYS) + bs
                for c in range(tm // LANES):
                    tok = slice(c * LANES, (c + 1) * LANES)
                    gate = jnp.zeros((groups, pack, LANES), BF16)
                    for h in range(PEER_HEADS):
                        keys = slice(h * groups, (h + 1) * groups)
                        grp = pl.ds(pl.multiple_of(h * PEER_NKEYS + e * SUBLANES, SUBLANES), SUBLANES)
                        n1 = jnp.broadcast_to(n1_ref[grp, tok][bi:bi + 1, :], (pack, LANES)).astype(BF16)
                        e1 = jnp.broadcast_to(e1_ref[grp, tok][bi:bi + 1, :], (pack, LANES)).astype(BF16)
                        gate += jnp.where(r2_ref[keys, :, tok] < n1[None], e2_ref[keys, :, tok], 0.0) * e1[None]
                    act = _gelu(at[bs * PEER_NKEYS:(bs + 1) * PEER_NKEYS, tok]).astype(BF16)
                    w_ref[slot, bi * groups:(bi + 1) * groups, :, tok] = act.reshape(groups, pack, LANES) * gate

    def accumulate(slot):
        w = w_ref[slot].reshape(PEER_EXPERT_TILE, tm)
        o_ref[...] += jnp.dot(vt_ref[...], w, preferred_element_type=F32)

    @pl.when(e == 0)
    def _():
        o_ref[...] = jnp.zeros_like(o_ref)
        gated_activations(0)

    @pl.when((e > 0) & (e < n_steps))
    def _():
        gated_activations(e % 2)
        accumulate((e - 1) % 2)

    @pl.when(e == n_steps)
    def _():
        accumulate((n_steps - 1) % 2)


def _peer_apply(h2, u, vt, tables):
    t = h2.shape[0]
    tm = min(PEER_TOKEN_TILE, t)
    eb = PEER_EXPERT_TILE
    n_steps = PEER_N // eb
    pack = 2 * SUBLANES
    r2, e2, n1, e1 = tables
    r2 = r2.reshape(PEER_HEADS * PEER_NKEYS // pack, pack, t)
    e2 = e2.reshape(PEER_HEADS * PEER_NKEYS // pack, pack, t)
    slab = pl.BlockSpec((PEER_HEADS * PEER_NKEYS // pack, pack, tm), lambda i, e: (0, 0, i))
    rowt = pl.BlockSpec((PEER_HEADS * PEER_NKEYS, tm), lambda i, e: (0, i))
    return pl.pallas_call(
        functools.partial(_peer_kernel, tm=tm, n_steps=n_steps),
        grid=(t // tm, n_steps + 1),
        in_specs=[pl.BlockSpec((tm, D_MODEL), lambda i, e: (i, 0)),
                  pl.BlockSpec((eb, D_MODEL), lambda i, e: (jnp.minimum(e, n_steps - 1), 0)),
                  pl.BlockSpec((D_MODEL, eb), lambda i, e: (0, jnp.maximum(e - 1, 0))),
                  slab, slab, rowt, rowt],
        out_specs=pl.BlockSpec((D_MODEL, tm), lambda i, e: (0, i)),
        out_shape=jax.ShapeDtypeStruct((D_MODEL, t), F32),
        scratch_shapes=[pltpu.VMEM((2, eb // pack, pack, tm), BF16)],
        compiler_params=_cparams("arbitrary", "arbitrary"),
        name="peer_apply",
    )(h2, u, vt, r2, e2, n1, e1)


def _final_kernel(x_ref, pt_ref, mod_ref, g_ref, b_ref, o_ref):
    y = DN_ALPHA * x_ref[...] + mod_ref[5:6, :] * pt_ref[...].T
    o_ref[...] = _ln(y) * g_ref[...] + b_ref[...]


def _finalize(x1, peer_t, mod, mod_base, rows_per_mod, ln_g, ln_b):
    t = x1.shape[0]
    tiles_per_mod = rows_per_mod // ROW_TILE
    row = pl.BlockSpec((ROW_TILE, D_MODEL), lambda i: (i, 0))
    vec = pl.BlockSpec((1, D_MODEL), lambda i: (0, 0))
    return pl.pallas_call(
        _final_kernel,
        grid=(t // ROW_TILE,),
        in_specs=[row,
                  pl.BlockSpec((D_MODEL, ROW_TILE), lambda i: (0, i)),
                  pl.BlockSpec((None, N_MOD, D_MODEL), lambda i: (mod_base + i // tiles_per_mod, 0, 0)),
                  vec, vec],
        out_specs=row,
        out_shape=jax.ShapeDtypeStruct((t, D_MODEL), F32),
        compiler_params=_cparams("arbitrary"),
        name="finalize",
    )(x1, peer_t, mod, ln_g, ln_b)


def _prep_layer(l, w_in, attn_sink, sgu_g, sgu_w, sgu_b, w_out, ln1_g, ln1_b,
                peer_wq, peer_keys, peer_u, peer_v, ln2_g, ln2_b):
    keys = peer_keys[l]
    zeros = jnp.zeros_like(keys[:, 0])
    kmat = jnp.concatenate([jnp.concatenate([keys[:, 0], zeros], axis=-1),
                            jnp.concatenate([zeros, keys[:, 1]], axis=-1)], axis=1)
    return dict(
        w_in=w_in[l].astype(BF16),
        sink_b=jnp.broadcast_to(
            jnp.pad(attn_sink[l].reshape(N_KV, GROUP), ((0, 0), (0, SUBLANES - GROUP)))[:, :, None],
            (N_KV, SUBLANES, LANES)),
        sgu_g=sgu_g[l].reshape(1, SGU_W),
        sgu_w=sgu_w[l].astype(BF16),
        sgu_bias=jnp.repeat(sgu_b[l].T, LANES, axis=1),
        w_out=w_out[l].astype(BF16),
        ln1_g=ln1_g[l].reshape(1, D_MODEL), ln1_b=ln1_b[l].reshape(1, D_MODEL),
        wq=peer_wq[l].astype(BF16),
        kmat=kmat.astype(BF16),
        u=peer_u[l].astype(BF16),
        vt=peer_v[l].T.astype(BF16),
        ln2_g=ln2_g[l].reshape(1, D_MODEL), ln2_b=ln2_b[l].reshape(1, D_MODEL),
    )


def _layer(x, batch, seq, mod, mod_base, rows_per_mod, lw, ctx):
    q, k, v, ug, vg, f = _in_projection(x, mod, mod_base, rows_per_mod, lw["w_in"])
    if ctx is None:
        attn = _context_attention(q, k, v, lw["sink_b"], batch, seq)
    else:
        cache_k, cache_v, layer, cos_t, sin_t = ctx
        attn = _latent_attention(q, k, v, cache_k, cache_v, layer, cos_t, sin_t, lw["sink_b"], batch, seq)
    sg = _spatial_gating(ug, vg, lw["sgu_g"], lw["sgu_w"], lw["sgu_bias"])
    fn = _fourier_mix(f, batch, seq)
    x1, h2, qp = _out_projection(x, attn, sg, fn, lw["w_out"], mod, mod_base, rows_per_mod,
                                 lw["ln1_g"], lw["ln1_b"], lw["wq"])
    tables = _peer_topk(qp, lw["kmat"])
    peer_t = _peer_apply(h2, lw["u"], lw["vt"], tables)
    x2 = _finalize(x1, peer_t, mod, mod_base, rows_per_mod, lw["ln2_g"], lw["ln2_b"])
    return x2, k, v


def kernel(x_prompt, x_sample, cache_k, cache_v, c, c_ctx, w_mod, b_mod, w_in, attn_sink, sgu_g, sgu_w, sgu_b, w_out, ln1_g, ln1_b, peer_wq, peer_keys, peer_u, peer_v, ln2_g, ln2_b):
    batch, seq, _ = x_prompt.shape
    dec_batch, dec_seq, _ = x_sample.shape
    past = cache_k.shape[2]
    assert 1 + dec_batch <= MOD_ROWS

    cvec = jnp.concatenate([c_ctx[None, :], c, jnp.zeros((MOD_ROWS - 1 - dec_batch, D_MODEL), F32)], axis=0)
    mod = _modulation(cvec, w_mod, b_mod)
    cos_t, sin_t = _rope_tables(dec_seq)
    ck = cache_k.reshape(dec_batch, DEPTH, past, KV_W)
    cv = cache_v.reshape(dec_batch, DEPTH, past, KV_W)

    xp = x_prompt.reshape(batch * seq, D_MODEL)
    xs = x_sample.reshape(dec_batch * dec_seq, D_MODEL)
    new_k, new_v = [], []
    for l in range(DEPTH):
        lw = _prep_layer(l, w_in, attn_sink, sgu_g, sgu_w, sgu_b, w_out, ln1_g, ln1_b,
                         peer_wq, peer_keys, peer_u, peer_v, ln2_g, ln2_b)
        xp, kl, vl = _layer(xp, batch, seq, mod[l], 0, batch * seq, lw, None)
        new_k.append(kl.reshape(batch, seq, N_KV, HEAD_DIM))
        new_v.append(vl.reshape(batch, seq, N_KV, HEAD_DIM))
        xs, _, _ = _layer(xs, dec_batch, dec_seq, mod[l], 1, dec_seq, lw, (ck, cv, l, cos_t, sin_t))
    return (xp.reshape(batch, seq, D_MODEL), xs.reshape(dec_batch, dec_seq, D_MODEL),
            jnp.stack(new_k, axis=1), jnp.stack(new_v, axis=1))
```

```python
import functools

import jax
import jax.numpy as jnp
from jax import lax
from jax.experimental import pallas as pl
from jax.experimental.pallas import tpu as pltpu

D_MODEL = 2048
DEPTH = 2
GRID_W = 64
HEAD_DIM = 128
N_HEADS = 8
N_KV = 2
GROUP = N_HEADS // N_KV
ATTN_W = N_HEADS * HEAD_DIM
KV_W = N_KV * HEAD_DIM
SGU_GROUPS = 4
SGU_W = 512
FNET_GROUPS = 4
FNET_W = 512
FNET_GD = 128
PROJ_W = ATTN_W + 2 * KV_W + 2 * SGU_W + FNET_W
BLOCK = 128
WINDOW = 128
CHUNK = 128
ROPE_BASE = 10000.0
N_MOD = 6
PEER_HEADS = 8
PEER_NKEYS = 128
PEER_N = PEER_NKEYS * PEER_NKEYS
PEER_TOPK = 16
PEER_QDIM = 128
PEER_HALF = PEER_QDIM // 2
DN_ALPHA = (2 * DEPTH) ** 0.25
LN_EPS = 1e-5
NEG_INF = -1e30

F32 = jnp.float32
BF16 = jnp.bfloat16
LANES = 128
SUBLANES = 8
V7X_VMEM_BYTES = 64 * 2 ** 20
VMEM_LIMIT = V7X_VMEM_BYTES - 8 * 2 ** 20
ROW_TILE = 256
PEER_TOKEN_TILE = 512
PEER_EXPERT_SUBTILE = 256
PEER_EXPERT_TILE = SUBLANES * PEER_NKEYS
MOD_COL_TILE = 1024
MOD_ROWS = 16

_NT = (((1,), (1,)), ((), ()))


def _cparams(*sem):
    return pltpu.CompilerParams(dimension_semantics=sem, vmem_limit_bytes=VMEM_LIMIT)


def _ln(x):
    mu = jnp.mean(x, axis=-1, keepdims=True)
    xc = x - mu
    var = jnp.mean(xc * xc, axis=-1, keepdims=True)
    return xc * lax.rsqrt(var + LN_EPS)


def _gelu(x):
    return 0.5 * x * (1.0 + lax.erf(x * (0.5 ** 0.5)))


def _pack_bf16(x):
    return pltpu.bitcast(x.astype(BF16), jnp.uint32)


def _unpack_bf16(words):
    return pltpu.bitcast(words, BF16)


def _pack_kernel(x_ref, o_ref, *, transpose):
    x = x_ref[...]
    o_ref[...] = _pack_bf16(x.T if transpose else x)


def _pack_table(table, layer, transpose):
    _, n, d = table.shape
    rows = PEER_EXPERT_TILE
    if transpose:
        out_spec = pl.BlockSpec((d // 2, rows), lambda e: (0, e))
        out_shape = jax.ShapeDtypeStruct((d // 2, n), jnp.uint32)
    else:
        out_spec = pl.BlockSpec((rows // 2, d), lambda e: (e, 0))
        out_shape = jax.ShapeDtypeStruct((n // 2, d), jnp.uint32)
    return pl.pallas_call(
        functools.partial(_pack_kernel, transpose=transpose),
        grid=(n // rows,),
        in_specs=[pl.BlockSpec((None, rows, d), lambda e: (layer, e, 0))],
        out_specs=out_spec,
        out_shape=out_shape,
        compiler_params=_cparams("arbitrary"),
        name="pack_table_t" if transpose else "pack_table",
    )(table)


def _mod_kernel(c_ref, w_ref, b_ref, o_ref):
    c = c_ref[...]
    a = (c / (1.0 + jnp.exp(-c))).astype(BF16)
    o_ref[...] = jnp.dot(a, w_ref[...].astype(BF16), preferred_element_type=F32) + b_ref[...]


def _modulation(cvec, w_mod, b_mod):
    n = N_MOD * D_MODEL
    out = pl.pallas_call(
        _mod_kernel,
        grid=(DEPTH, n // MOD_COL_TILE),
        in_specs=[
            pl.BlockSpec((MOD_ROWS, D_MODEL), lambda l, j: (0, 0)),
            pl.BlockSpec((None, D_MODEL, MOD_COL_TILE), lambda l, j: (l, 0, j)),
            pl.BlockSpec((None, 1, MOD_COL_TILE), lambda l, j: (l, 0, j)),
        ],
        out_specs=pl.BlockSpec((None, MOD_ROWS, MOD_COL_TILE), lambda l, j: (l, 0, j)),
        out_shape=jax.ShapeDtypeStruct((DEPTH, MOD_ROWS, n), F32),
        compiler_params=_cparams("arbitrary", "arbitrary"),
        name="modulation",
    )(cvec, w_mod, b_mod.reshape(DEPTH, 1, n))
    return out.reshape(DEPTH, MOD_ROWS, N_MOD, D_MODEL)


_PROJ_SLICES = ((0, ATTN_W), (ATTN_W, KV_W), (ATTN_W + KV_W, KV_W),
                (ATTN_W + 2 * KV_W, SGU_W), (ATTN_W + 2 * KV_W + SGU_W, SGU_W),
                (ATTN_W + 2 * KV_W + 2 * SGU_W, FNET_W))


def _proj_kernel(x_ref, mod_ref, w_ref, *out_refs):
    h = _ln(x_ref[...]) * (1.0 + mod_ref[1:2, :]) + mod_ref[0:1, :]
    h = h.astype(BF16)
    for (start, width), o_ref in zip(_PROJ_SLICES, out_refs):
        o_ref[...] = jnp.dot(h, w_ref[:, start:start + width], preferred_element_type=F32)


def _in_projection(x, mod, mod_base, rows_per_mod, w_in):
    t = x.shape[0]
    tiles_per_mod = rows_per_mod // ROW_TILE
    row = lambda w: pl.BlockSpec((ROW_TILE, w), lambda i: (i, 0))
    return pl.pallas_call(
        _proj_kernel,
        grid=(t // ROW_TILE,),
        in_specs=[
            row(D_MODEL),
            pl.BlockSpec((None, N_MOD, D_MODEL), lambda i: (mod_base + i // tiles_per_mod, 0, 0)),
            pl.BlockSpec((D_MODEL, PROJ_W), lambda i: (0, 0)),
        ],
        out_specs=[row(w) for _, w in _PROJ_SLICES],
        out_shape=[jax.ShapeDtypeStruct((t, w), F32) for _, w in _PROJ_SLICES],
        compiler_params=_cparams("arbitrary"),
        name="in_projection",
    )(x, mod, w_in)


def _ctx_attn_kernel(q_ref, k_ref, v_ref, sink_ref, o_ref):
    k =k_ref[...].astype(BF16)
    v = v_ref[...].astype(BF16)
    scale = HEAD_DIM ** -0.5
    for g in range(GROUP):
        cols = slice(g * HEAD_DIM, (g + 1) * HEAD_DIM)
        q = q_ref[:, cols].astype(BF16)
        s = lax.dot_general(q, k, _NT, preferred_element_type=F32) * scale
        sk = sink_ref[g:g + 1, 0:1]
        m = jnp.maximum(jnp.max(s, axis=-1, keepdims=True), sk)
        p = jnp.exp(s - m)
        den = jnp.sum(p, axis=-1, keepdims=True) + jnp.exp(sk - m)
        o = jnp.dot(p.astype(BF16), v, preferred_element_type=F32)
        o_ref[:, cols] = o / den


def _context_attention(q, k, v, sink_b, batch, seq):
    t = q.shape[0]
    qspec = pl.BlockSpec((seq, GROUP * HEAD_DIM), lambda b, h: (b, h))
    kvspec = pl.BlockSpec((seq, HEAD_DIM), lambda b, h: (b, h))
    return pl.pallas_call(
        _ctx_attn_kernel,
        grid=(batch, N_KV),
        in_specs=[qspec, kvspec, kvspec, pl.BlockSpec((None, SUBLANES, LANES), lambda b, h: (h, 0, 0))],
        out_specs=qspec,
        out_shape=jax.ShapeDtypeStruct((t, ATTN_W), F32),
        compiler_params=_cparams("arbitrary", "arbitrary"),
        name="context_attention",
    )(q, k, v, sink_b)


def _lat_attn_kernel(q_ref, k_ref, v_ref, ck_ref, cv_ref, cos_ref, sin_ref, sink_ref, o_ref,
                     kp_ref, vp_ref, *, seq):
    scale = HEAD_DIM ** -0.5
    lane = lax.broadcasted_iota(jnp.int32, (1, HEAD_DIM), 1)
    first = (lane % 64) < 32

    def rope(x, cos, sin):
        sw = jnp.where(first, pltpu.roll(x, 96, 1), pltpu.roll(x, 32, 1))
        return x * cos + sw * sin

    pad = jnp.zeros((BLOCK, HEAD_DIM), BF16)
    kp_ref[0:BLOCK, :] = pad
    kp_ref[seq + BLOCK:seq + 2 * BLOCK, :] = pad
    vp_ref[0:BLOCK, :] = pad
    vp_ref[seq + BLOCK:seq + 2 * BLOCK, :] = pad
    kp_ref[BLOCK:seq + BLOCK, :] = rope(k_ref[...], cos_ref[...], sin_ref[...]).astype(BF16)
    vp_ref[BLOCK:seq + BLOCK, :] = v_ref[...].astype(BF16)
    ck = ck_ref[...].astype(BF16)
    cv = cv_ref[...].astype(BF16)

    def body(qb, carry):
        r0 = pl.multiple_of(qb * BLOCK, BLOCK)
        kb = kp_ref[pl.ds(r0, 3 * BLOCK), :]
        vb = vp_ref[pl.ds(r0, 3 * BLOCK), :]
        cosq = cos_ref[pl.ds(r0, BLOCK), :]
        sinq = sin_ref[pl.ds(r0, BLOCK), :]
        r = lax.broadcasted_iota(jnp.int32, (BLOCK, 3 * BLOCK), 0)
        j = lax.broadcasted_iota(jnp.int32, (BLOCK, 3 * BLOCK), 1)
        kpos = j + (r0 - BLOCK)
        valid = (j >= r) & (j <= r + 2 * WINDOW) & (kpos >= 0) & (kpos < seq)
        for g in range(GROUP):
            cols = slice(g * HEAD_DIM, (g + 1) * HEAD_DIM)
            q = rope(q_ref[pl.ds(r0, BLOCK), cols], cosq, sinq).astype(BF16)
            sl = lax.dot_general(q, kb, _NT, preferred_element_type=F32) * scale
            sl = jnp.where(valid, sl, NEG_INF)
            sc = lax.dot_general(q, ck, _NT, preferred_element_type=F32) * scale
            sk = sink_ref[g:g + 1, 0:1]
            m = jnp.maximum(jnp.maximum(jnp.max(sl, axis=-1, keepdims=True),
                                        jnp.max(sc, axis=-1, keepdims=True)), sk)
            p_l = jnp.exp(sl - m)
            p_c = jnp.exp(sc - m)
            den = (jnp.sum(p_l, axis=-1, keepdims=True) + jnp.sum(p_c, axis=-1, keepdims=True)
                   + jnp.exp(sk - m))
            o = (jnp.dot(p_l.astype(BF16), vb, preferred_element_type=F32)
                 + jnp.dot(p_c.astype(BF16), cv, preferred_element_type=F32))
            o_ref[pl.ds(r0, BLOCK), cols] = o / den
        return carry

    lax.fori_loop(0, seq // BLOCK, body, 0)


def _latent_attention(q, k, v, cache_k, cache_v, layer, cos_t, sin_t, sink_b, batch, seq):
    t = q.shape[0]
    past = cache_k.shape[2]
    qspec = pl.BlockSpec((seq, GROUP * HEAD_DIM), lambda b, h: (b, h))
    kvspec = pl.BlockSpec((seq, HEAD_DIM), lambda b, h: (b, h))
    cspec = pl.BlockSpec((None, None, past, HEAD_DIM), lambda b, h: (b, layer, 0, h))
    tspec = pl.BlockSpec((seq, HEAD_DIM), lambda b, h: (0, 0))
    return pl.pallas_call(
        functools.partial(_lat_attn_kernel, seq=seq),
        grid=(batch, N_KV),
        in_specs=[qspec, kvspec, kvspec, cspec, cspec, tspec, tspec,
                  pl.BlockSpec((None, SUBLANES, LANES), lambda b, h: (h, 0, 0))],
        out_specs=qspec,
        out_shape=jax.ShapeDtypeStruct((t, ATTN_W), F32),
        scratch_shapes=[pltpu.VMEM((seq + 2 * BLOCK, HEAD_DIM), BF16),
                        pltpu.VMEM((seq + 2 * BLOCK, HEAD_DIM), BF16)],
        compiler_params=_cparams("arbitrary", "arbitrary"),
        name="latent_attention",
    )(q, k, v, cache_k, cache_v, cos_t, sin_t, sink_b)


def _rope_tables(seq):
    pos = jnp.arange(seq)
    r = (pos // GRID_W).astype(F32)
    col = (pos % GRID_W).astype(F32)
    n = HEAD_DIM // 4
    inv = ROPE_BASE ** (-jnp.arange(n, dtype=F32) / n)
    ang_r = r[:, None] * inv
    ang_c = col[:, None] * inv
    cos_t = jnp.concatenate([jnp.cos(ang_r), jnp.cos(ang_r), jnp.cos(ang_c), jnp.cos(ang_c)], axis=-1)
    sin_t = jnp.concatenate([-jnp.sin(ang_r), jnp.sin(ang_r), -jnp.sin(ang_c), jnp.sin(ang_c)], axis=-1)
    return cos_t, sin_t


def _sgu_kernel(ug_ref, vg_ref, g_ref, ws_ref, bias_ref, o_ref):
    for c in range(ROW_TILE // CHUNK):
        rows = slice(c * CHUNK, (c + 1) * CHUNK)
        for g in range(SGU_GROUPS):
            cols = slice(g * LANES, (g + 1) * LANES)
            u = _gelu(ug_ref[rows, cols])
            vn = _ln(_gelu(vg_ref[rows, cols])) * g_ref[0:1, cols]
            mixed = jnp.dot(ws_ref[g], vn.astype(BF16), preferred_element_type=F32) + bias_ref[:, cols]
            o_ref[rows, cols] = u * mixed


def _spatial_gating(ug, vg, sgu_g, sgu_w, sgu_bias):
    t = ug.shape[0]
    row = pl.BlockSpec((ROW_TILE, SGU_W), lambda i: (i, 0))
    return pl.pallas_call(
        _sgu_kernel,
        grid=(t // ROW_TILE,),
        in_specs=[row, row,
                  pl.BlockSpec((1, SGU_W), lambda i: (0, 0)),
                  pl.BlockSpec((SGU_GROUPS, CHUNK, CHUNK), lambda i: (0, 0, 0)),
                  pl.BlockSpec((CHUNK, SGU_W), lambda i: (0, 0))],
        out_specs=row,
        out_shape=jax.ShapeDtypeStruct((t, SGU_W), F32),
        compiler_params=_cparams("arbitrary"),
        name="spatial_gating",
    )(ug, vg, sgu_g, sgu_w, sgu_bias)


def _fnet_kernel(f_ref, cc_ref, cs_ref, o_ref, z_ref, *, seq):
    @pl.when(pl.program_id(1) == 0)
    def _():
        cc = cc_ref[...]
        for g in range(FNET_GROUPS):
            cols = slice(g * FNET_GD, (g + 1) * FNET_GD)
            zz = jnp.dot(f_ref[:, cols].astype(BF16), cc, preferred_element_type=F32)
            z_ref[0:seq, cols] = zz[:, :FNET_GD].astype(BF16)
            z_ref[seq:2 * seq, cols] = zz[:, FNET_GD:].astype(BF16)

    y = jnp.dot(cs_ref[...], z_ref[...], preferred_element_type=F32)
    o_ref[...] = y * ((seq * FNET_GD) ** -0.5)


def _dft_tables(n):
    k = jnp.arange(n, dtype=jnp.int32)
    ang = ((k[:, None] * k[None, :]) % n).astype(F32) * (2.0 * jnp.pi / n)
    return jnp.cos(ang), jnp.sin(ang)


def _fourier_mix(f, batch, seq):
    t = f.shape[0]
    tr = min(seq, 512)
    c_c, s_c = _dft_tables(FNET_GD)
    c_s, s_s = _dft_tables(seq)
    cc = jnp.concatenate([c_c, s_c], axis=1).astype(BF16)
    cs = jnp.concatenate([c_s, -s_s], axis=1).astype(BF16)
    nj = seq // tr
    return pl.pallas_call(
        functools.partial(_fnet_kernel, seq=seq),
        grid=(batch, nj),
        in_specs=[pl.BlockSpec((seq, FNET_W), lambda b, j: (b, 0)),
                  pl.BlockSpec((FNET_GD, 2 * FNET_GD), lambda b, j: (0, 0)),
                  pl.BlockSpec((tr, 2 * seq), lambda b, j: (j, 0))],
        out_specs=pl.BlockSpec((tr, FNET_W), lambda b, j: (b * nj + j, 0)),
        out_shape=jax.ShapeDtypeStruct((t, FNET_W), F32),
        scratch_shapes=[pltpu.VMEM((2 * seq, FNET_W), BF16)],
        compiler_params=_cparams("arbitrary", "arbitrary"),
        name="fourier_mix",
    )(f, cc, cs)


def _mix_kernel(x_ref, a_ref, s_ref, f_ref, wo_ref, mod_ref, g1_ref, b1_ref, wq_ref,
                x1_ref, h2_ref, qp_ref):
    mix = jnp.dot(a_ref[...].astype(BF16), wo_ref[0:ATTN_W, :], preferred_element_type=F32)
    mix += jnp.dot(s_ref[...].astype(BF16), wo_ref[ATTN_W:ATTN_W + SGU_W, :], preferred_element_type=F32)
    mix += jnp.dot(f_ref[...].astype(BF16), wo_ref[ATTN_W + SGU_W:, :], preferred_element_type=F32)
    x1 = _ln(DN_ALPHA * x_ref[...] + mod_ref[2:3, :] * mix) * g1_ref[...] + b1_ref[...]
    x1_ref[...] = x1
    h2 = _ln(x1) * (1.0 + mod_ref[4:5, :]) + mod_ref[3:4, :]
    h2_ref[...] = _pack_bf16(h2.T)
    q = jnp.dot(h2.astype(BF16), wq_ref[...], preferred_element_type=F32)
    for h in range(PEER_HEADS):
        qp_ref[h] = q[:, h * PEER_QDIM:(h + 1) * PEER_QDIM].astype(BF16)


def _out_projection(x, attn, sg, fn, w_out, mod, mod_base, rows_per_mod, ln_g, ln_b, wq):
    t = x.shape[0]
    tiles_per_mod = rows_per_mod // ROW_TILE
    row = lambda w: pl.BlockSpec((ROW_TILE, w), lambda i: (i, 0))
    full = lambda a, b: pl.BlockSpec((a, b), lambda i: (0, 0))
    return pl.pallas_call(
        _mix_kernel,
        grid=(t // ROW_TILE,),
        in_specs=[row(D_MODEL), row(ATTN_W), row(SGU_W), row(FNET_W),
                  full(D_MODEL, D_MODEL),
                  pl.BlockSpec((None, N_MOD, D_MODEL), lambda i: (mod_base + i // tiles_per_mod, 0, 0)),
                  full(1, D_MODEL), full(1, D_MODEL),
                  full(D_MODEL, PEER_HEADS * PEER_QDIM)],
        out_specs=[row(D_MODEL), pl.BlockSpec((D_MODEL // 2, ROW_TILE), lambda i: (0, i)),
                   pl.BlockSpec((PEER_HEADS, ROW_TILE, PEER_QDIM), lambda i: (0, i, 0))],
        out_shape=[jax.ShapeDtypeStruct((t, D_MODEL), F32),
                   jax.ShapeDtypeStruct((D_MODEL // 2, t), jnp.uint32),
                   jax.ShapeDtypeStruct((PEER_HEADS, t, PEER_QDIM), BF16)],
        compiler_params=_cparams("arbitrary"),
        name="out_projection",
    )(x, attn, sg, fn, w_out, mod, ln_g, ln_b, wq)


def _pair_candidates(v1, v2):
    half = PEER_TOPK // 2
    pieces = [v1[0:1] + v2] + [v1[r1:r1 + 1] + v2[0:half] for r1 in range(1, PEER_TOPK)]
    pos_pieces = [lax.broadcasted_iota(jnp.int32, (PEER_TOPK, LANES), 0)]
    for r1 in range(1, PEER_TOPK):
        pos_pieces.append(lax.broadcasted_iota(jnp.int32, (half, LANES), 0) + r1 * PEER_TOPK)
    spans = [(0, PEER_TOPK)] + [(PEER_TOPK + (r1 - 1) * half, half) for r1 in range(1, PEER_TOPK)]
    return jnp.concatenate(pieces, axis=0), jnp.concatenate(pos_pieces, axis=0).astype(F32), spans


def _descending_maxima(slabs):
    w = slabs[0].shape[1]
    slot = lax.broadcasted_iota(jnp.int32, (PEER_TOPK, w), 0)

    def body(k, carry):
        out = []
        for s, (prev, vals) in zip(slabs, carry):
            m = jnp.max(jnp.where(s < prev, s, -jnp.inf), axis=0, keepdims=True)
            out.append((m, jnp.where(slot == k, m, vals)))
        return tuple(out)

    init = tuple((jnp.full((1, w), jnp.inf, F32), jnp.zeros((PEER_TOPK, w), F32)) for _ in slabs)
    return [vals for _, vals in lax.fori_loop(0, PEER_TOPK, body, init)]


def _count_rows(mask):
    return jnp.sum(jnp.where(mask, 1.0, 0.0), axis=0, keepdims=True)


def _topk_tables_distinct(s1, s2):
    nc = len(s1)
    k = float(PEER_TOPK)
    vals = _descending_maxima(s1 + s2)
    v1, v2 = vals[:nc], vals[nc:]
    cands = [_pair_candidates(a, b) for a, b in zip(v1, v2)]
    best = _descending_maxima([cand for cand, _, _ in cands])
    tables, bad = [], jnp.zeros((1, LANES), F32)
    for c in range(nc):
        cand, _, spans = cands[c]
        taken = jnp.where(cand >= best[c][PEER_TOPK - 1:PEER_TOPK], 1.0, 0.0)
        cnt = [jnp.sum(taken[lo:lo + n], axis=0, keepdims=True) for lo, n in spans]
        z = jnp.sum(jnp.exp(best[c] - best[c][0:1]), axis=0, keepdims=True)
        rank2 = jnp.zeros((PEER_NKEYS, LANES), F32)
        n1 = jnp.zeros((PEER_NKEYS, LANES), F32)
        for r in range(PEER_TOPK):
            rank2 = rank2 + jnp.where(v2[c][r:r + 1] > s2[c], 1.0, 0.0)
            n1 = jnp.where(s1[c] == v1[c][r:r + 1], cnt[r], n1)
        tables.append((rank2, jnp.exp(s2[c] - v2[c][0:1]), n1, jnp.exp(s1[c] - v1[c][0:1]) / z))
        for s, v in ((s1[c], v1[c]), (s2[c], v2[c])):
            bad = bad + jnp.abs(_count_rows(s >= v[PEER_TOPK - 1:PEER_TOPK]) - k)
        bad = bad + jnp.abs(sum(cnt) - k)
    return tables, bad


def _top16_rows(scores):
    n, w = scores[0].shape
    row = lax.broadcasted_iota(jnp.int32, (n, w), 0).astype(F32)
    slot = lax.broadcasted_iota(jnp.int32, (PEER_TOPK, w), 0)

    def body(k, carry):
        kf = lax.convert_element_type(k, F32)
        out = []
        for cur, rank, vals in carry:
            m = jnp.max(cur, axis=0, keepdims=True)
            hit = row == jnp.min(jnp.where(cur == m, row, float(n)), axis=0, keepdims=True)
            out.append((jnp.where(hit, -jnp.inf, cur), jnp.where(hit, kf, rank),
                        jnp.where(slot == k, m, vals)))
        return tuple(out)

    init = tuple((s, jnp.full((n, w), float(PEER_TOPK), F32), jnp.zeros((PEER_TOPK, w), F32))
                 for s in scores)
    res = lax.fori_loop(0, PEER_TOPK, body, init)
    return [(rank, vals) for _, rank, vals in res]


def _topk_tables_exact(s1, s2):
    nc = len(s1)
    tops = _top16_rows(s1 + s2)
    slot = lax.broadcasted_iota(jnp.int32, (PEER_TOPK, LANES), 0).astype(F32)
    cands = [_pair_candidates(tops[c][1], tops[nc + c][1]) for c in range(nc)]

    def body(k, carry):
        out = []
        for (cand, cnt, z), (cand0, pos, _) in zip(carry, cands):
            m = jnp.max(cand, axis=0, keepdims=True)
            p = jnp.min(jnp.where(cand == m, pos, 4096.0), axis=0, keepdims=True)
            out.append((jnp.where(pos == p, -jnp.inf, cand),
                        cnt + jnp.where(slot == jnp.floor(p * (1.0 / PEER_TOPK)), 1.0, 0.0),
                        z + jnp.exp(m - cand0[0:1])))
        return tuple(out)

    init = tuple((cand, jnp.zeros((PEER_TOPK, LANES), F32), jnp.zeros((1, LANES), F32))
                 for cand, _, _ in cands)
    res = lax.fori_loop(0, PEER_TOPK, body, init)
    tables = []
    for c in range(nc):
        (rank1, v1), (rank2, v2) = tops[c], tops[nc + c]
        _, cnt, z = res[c]
        n1 = jnp.zeros((PEER_NKEYS, LANES), F32)
        for r1 in range(PEER_TOPK):
            n1 = jnp.where(rank1 == float(r1), cnt[r1:r1 + 1], n1)
        tables.append((rank2, jnp.exp(s2[c] - v2[0:1]), n1, jnp.exp(s1[c] - v1[0:1]) / z))
    return tables


def _topk_kernel(q_ref, km_ref, r2_ref, e2_ref, n1_ref, e1_ref, *, tm):
    km = km_ref[...]
    toks = [slice(c * LANES, (c + 1) * LANES) for c in range(tm // LANES)]
    sts = [lax.dot_general(km, q_ref[tok, :], _NT, preferred_element_type=F32) for tok in toks]
    s1 = [st[0:PEER_NKEYS] for st in sts]
    s2 = [st[PEER_NKEYS:2 * PEER_NKEYS] for st in sts]

    def store(tables):
        for tok, (rank2, e2, n1, e1) in zip(toks, tables):
            r2_ref[:, tok] = _pack_bf16(rank2)
            e2_ref[:, tok] = _pack_bf16(e2)
            n1_ref[:, tok] = n1
            e1_ref[:, tok] = e1

    tables, bad = _topk_tables_distinct(s1, s2)
    store(tables)

    @pl.when(jnp.max(bad) > 0.0)
    def _():
        store(_topk_tables_exact(s1, s2))


def _peer_topk(qp, kmat):
    t = qp.shape[1]
    tm = ROW_TILE
    slab_spec = pl.BlockSpec((PEER_NKEYS // 2, tm), lambda i, h: (h, i))
    slab_shape = jax.ShapeDtypeStruct((PEER_HEADS * PEER_NKEYS // 2, t), jnp.uint32)
    row_spec = pl.BlockSpec((PEER_NKEYS, tm), lambda i, h: (h, i))
    row_shape = jax.ShapeDtypeStruct((PEER_HEADS * PEER_NKEYS, t), F32)
    return pl.pallas_call(
        functools.partial(_topk_kernel, tm=tm),
        grid=(t // tm, PEER_HEADS),
        in_specs=[pl.BlockSpec((None, tm, PEER_QDIM), lambda i, h: (h, i, 0)),
                  pl.BlockSpec((None, 2 * PEER_NKEYS, PEER_QDIM), lambda i, h: (h, 0, 0))],
        out_specs=[slab_spec, slab_spec, row_spec, row_spec],
        out_shape=[slab_shape, slab_shape, row_shape, row_shape],
        compiler_params=_cparams("arbitrary", "arbitrary"),
        name="peer_topk",
    )(qp, kmat)


def _peer_kernel(h_ref, u_ref, vt_ref, r2_ref, e2_ref, n1_ref, e1_ref, o_ref, w_ref, *, tm):
    e = pl.program_id(1)
    words = PEER_NKEYS // 2

    @pl.when(e == 0)
    def _():
        o_ref[...] = jnp.zeros_like(o_ref)

    ht = _unpack_bf16(h_ref[...])
    for sb in range(PEER_EXPERT_TILE // PEER_EXPERT_SUBTILE):
        urows = slice(sb * PEER_EXPERT_SUBTILE // 2, (sb + 1) * PEER_EXPERT_SUBTILE // 2)
        at = jnp.dot(_unpack_bf16(u_ref[urows, :]), ht, preferred_element_type=F32)
        for bs in range(PEER_EXPERT_SUBTILE // PEER_NKEYS):
            bi = sb * (PEER_EXPERT_SUBTILE // PEER_NKEYS) + bs
            rows = slice(bi * words, (bi + 1) * words)
            for c in range(tm // LANES):
                tok = slice(c * LANES, (c + 1) * LANES)
                gate = jnp.zeros((PEER_NKEYS, LANES), BF16)
                for h in range(PEER_HEADS):
                    keys = slice(h * words, (h + 1) * words)
                    grp = pl.ds(pl.multiple_of(h * PEER_NKEYS + e * SUBLANES, SUBLANES), SUBLANES)
                    n1 = jnp.broadcast_to(n1_ref[grp, tok][bi:bi + 1, :], (PEER_NKEYS, LANES)).astype(BF16)
                    e1 = jnp.broadcast_to(e1_ref[grp, tok][bi:bi + 1, :], (PEER_NKEYS, LANES)).astype(BF16)
                    r2 = _unpack_bf16(r2_ref[keys, tok])
                    e2 = _unpack_bf16(e2_ref[keys, tok])
                    gate += jnp.where(r2 < n1, e2, 0.0) * e1
                act = _gelu(at[bs * PEER_NKEYS:(bs + 1) * PEER_NKEYS, tok]).astype(BF16)
                w_ref[rows, tok] = pltpu.bitcast(act * gate, jnp.uint32)
    o_ref[...] += jnp.dot(_unpack_bf16(vt_ref[...]), _unpack_bf16(w_ref[...]), preferred_element_type=F32)


def _peer_apply(h2t, u, vt, tables):
    t = h2t.shape[1]
    tm = min(PEER_TOKEN_TILE, t)
    eb = PEER_EXPERT_TILE
    slab = pl.BlockSpec((PEER_HEADS * PEER_NKEYS // 2, tm), lambda i, e: (0, i))
    rowt = pl.BlockSpec((PEER_HEADS * PEER_NKEYS, tm), lambda i, e: (0, i))
    return pl.pallas_call(
        functools.partial(_peer_kernel, tm=tm),
        grid=(t // tm, PEER_N // eb),
        in_specs=[pl.BlockSpec((D_MODEL // 2, tm), lambda i, e: (0, i)),
                  pl.BlockSpec((eb // 2, D_MODEL), lambda i, e: (e, 0)),
                  pl.BlockSpec((D_MODEL // 2, eb), lambda i, e: (0, e)),
                  slab, slab, rowt, rowt],
        out_specs=pl.BlockSpec((D_MODEL, tm), lambda i, e: (0, i)),
        out_shape=jax.ShapeDtypeStruct((D_MODEL, t), F32),
        scratch_shapes=[pltpu.VMEM((eb // 2, tm), jnp.uint32)],
        compiler_params=_cparams("arbitrary", "arbitrary"),
        name="peer_apply",
    )(h2t, u, vt, *tables)


def _final_kernel(x_ref, pt_ref, mod_ref, g_ref, b_ref, o_ref):
    y = DN_ALPHA * x_ref[...] + mod_ref[5:6, :] * pt_ref[...].T
    o_ref[...] = _ln(y) * g_ref[...] + b_ref[...]


def _finalize(x1, peer_t, mod, mod_base, rows_per_mod, ln_g, ln_b):
    t = x1.shape[0]
    tiles_per_mod = rows_per_mod // ROW_TILE
    row = pl.BlockSpec((ROW_TILE, D_MODEL), lambda i: (i, 0))
    vec = pl.BlockSpec((1, D_MODEL), lambda i: (0, 0))
    return pl.pallas_call(
        _final_kernel,
        grid=(t // ROW_TILE,),
        in_specs=[row,
                  pl.BlockSpec((D_MODEL, ROW_TILE), lambda i: (0, i)),
                  pl.BlockSpec((None, N_MOD, D_MODEL), lambda i: (mod_base + i // tiles_per_mod, 0, 0)),
                  vec, vec],
        out_specs=row,
        out_shape=jax.ShapeDtypeStruct((t, D_MODEL), F32),
        compiler_params=_cparams("arbitrary"),
        name="finalize",
    )(x1, peer_t, mod, ln_g, ln_b)


def _prep_layer(l, w_in, attn_sink, sgu_g, sgu_w, sgu_b, w_out, ln1_g, ln1_b,
                peer_wq, peer_keys, peer_u, peer_v, ln2_g, ln2_b):
    keys = peer_keys[l]
    zeros = jnp.zeros_like(keys[:, 0])
    kmat = jnp.concatenate([jnp.concatenate([keys[:, 0], zeros], axis=-1),
                            jnp.concatenate([zeros, keys[:, 1]], axis=-1)], axis=1)
    return dict(
        w_in=w_in[l].astype(BF16),
        sink_b=jnp.broadcast_to(
            jnp.pad(attn_sink[l].reshape(N_KV, GROUP), ((0, 0), (0, SUBLANES - GROUP)))[:, :, None],
            (N_KV, SUBLANES, LANES)),
        sgu_g=sgu_g[l].reshape(1, SGU_W),
        sgu_w=sgu_w[l].astype(BF16),
        sgu_bias=jnp.repeat(sgu_b[l].T, LANES, axis=1),
        w_out=w_out[l].astype(BF16),
        ln1_g=ln1_g[l].reshape(1, D_MODEL), ln1_b=ln1_b[l].reshape(1, D_MODEL),
        wq=peer_wq[l].astype(BF16),
        kmat=kmat.astype(BF16),
        u=_pack_table(peer_u, l, transpose=False),
        vt=_pack_table(peer_v, l, transpose=True),
        ln2_g=ln2_g[l].reshape(1, D_MODEL), ln2_b=ln2_b[l].reshape(1, D_MODEL),
    )


def _layer(x, batch, seq, mod, mod_base, rows_per_mod, lw, ctx):
    q, k, v, ug, vg, f = _in_projection(x, mod, mod_base, rows_per_mod, lw["w_in"])
    if ctx is None:
        attn = _context_attention(q, k, v, lw["sink_b"], batch, seq)
    else:
        cache_k, cache_v, layer, cos_t, sin_t = ctx
        attn = _latent_attention(q, k, v, cache_k, cache_v, layer, cos_t, sin_t, lw["sink_b"], batch, seq)
    sg = _spatial_gating(ug, vg, lw["sgu_g"], lw["sgu_w"], lw["sgu_bias"])
    fn = _fourier_mix(f, batch, seq)
    x1, h2, qp = _out_projection(x, attn, sg, fn, lw["w_out"], mod, mod_base, rows_per_mod,
                                 lw["ln1_g"], lw["ln1_b"], lw["wq"])
    tables = _peer_topk(qp, lw["kmat"])
    peer_t = _peer_apply(h2, lw["u"], lw["vt"], tables)
    x2 = _finalize(x1, peer_t, mod, mod_base, rows_per_mod, lw["ln2_g"], lw["ln2_b"])
    return x2, k, v


def kernel(x_prompt, x_sample, cache_k, cache_v, c, c_ctx, w_mod, b_mod, w_in, attn_sink, sgu_g, sgu_w, sgu_b, w_out, ln1_g, ln1_b, peer_wq, peer_keys, peer_u, peer_v, ln2_g, ln2_b):
    batch, seq, _ = x_prompt.shape
    dec_batch, dec_seq, _ = x_sample.shape
    past = cache_k.shape[2]
    assert 1 + dec_batch <= MOD_ROWS

    cvec = jnp.concatenate([c_ctx[None, :], c, jnp.zeros((MOD_ROWS - 1 - dec_batch, D_MODEL), F32)], axis=0)
    mod = _modulation(cvec, w_mod, b_mod)
    cos_t, sin_t = _rope_tables(dec_seq)
    ck = cache_k.reshape(dec_batch, DEPTH, past, KV_W)
    cv = cache_v.reshape(dec_batch, DEPTH, past, KV_W)

    xp = x_prompt.reshape(batch * seq, D_MODEL)
    xs = x_sample.reshape(dec_batch * dec_seq, D_MODEL)
    new_k, new_v = [], []
    for l in range(DEPTH):
        lw = _prep_layer(l, w_in, attn_sink, sgu_g, sgu_w, sgu_b, w_out, ln1_g, ln1_b,
                         peer_wq, peer_keys, peer_u, peer_v, ln2_g, ln2_b)
        xp, kl, vl = _layer(xp, batch, seq, mod[l], 0, batch * seq, lw, None)
        new_k.append(kl.reshape(batch, seq, N_KV, HEAD_DIM))
        new_v.append(vl.reshape(batch, seq, N_KV, HEAD_DIM))
        xs, _, _ = _layer(xs, dec_batch, dec_seq, mod[l], 1, dec_seq, lw, (ck, cv, l, cos_t, sin_t))
    return (xp.reshape(batch, seq, D_MODEL), xs.reshape(dec_batch, dec_seq, D_MODEL),
            jnp.stack(new_k, axis=1), jnp.stack(new_v, axis=1))
```

```python
import functools

import jax
import jax.numpy as jnp
from jax import lax
from jax.experimental import pallas as pl
from jax.experimental.pallas import tpu as pltpu

D_MODEL = 2048
DEPTH = 2
GRID_W = 64
HEAD_DIM = 128
N_HEADS = 8
N_KV = 2
GROUP = N_HEADS // N_KV
ATTN_W = N_HEADS * HEAD_DIM
KV_W = N_KV * HEAD_DIM
SGU_GROUPS = 4
SGU_W = 512
FNET_GROUPS = 4
FNET_W = 512
FNET_GD = 128
PROJ_W = ATTN_W + 2 * KV_W + 2 * SGU_W + FNET_W
BLOCK = 128
WINDOW = 128
CHUNK = 128
ROPE_BASE = 10000.0
N_MOD = 6
PEER_HEADS = 8
PEER_NKEYS = 128
PEER_N = PEER_NKEYS * PEER_NKEYS
PEER_TOPK = 16
PEER_QDIM = 128
PEER_HALF = PEER_QDIM // 2
DN_ALPHA = (2 * DEPTH) ** 0.25
LN_EPS = 1e-5
NEG_INF = -1e30

F32 = jnp.float32
BF16 = jnp.bfloat16
LANES = 128
SUBLANES = 8
V7X_VMEM_BYTES = 64 * 2 ** 20
VMEM_LIMIT = V7X_VMEM_BYTES - 8 * 2 ** 20
ROW_TILE = 256
TOPK_TOKEN_TILE = 512
PEER_TOKEN_TILE = 512
PEER_EXPERT_SUBTILE = 256
PEER_EXPERT_TILE = SUBLANES * PEER_NKEYS
MOD_COL_TILE = 1024
MOD_ROWS = 16

_NT = (((1,), (1,)), ((), ()))


def _cparams(*sem):
    return pltpu.CompilerParams(dimension_semantics=sem, vmem_limit_bytes=VMEM_LIMIT)


def _row_reduce(op, lane_op, *blocks):
    acc = None
    for x in blocks:
        for i in range(x.shape[1] // LANES):
            chunk = x[:, i * LANES:(i + 1) * LANES]
            acc = chunk if acc is None else op(acc, chunk)
    return lane_op(acc, axis=-1, keepdims=True)


def _ln(x):
    inv_n = 1.0 / x.shape[-1]
    xc = x - _row_reduce(jnp.add, jnp.sum, x) * inv_n
    var = _row_reduce(jnp.add, jnp.sum, xc * xc) * inv_n
    return xc * lax.rsqrt(var + LN_EPS)


def _gelu(x):
    return 0.5 * x * (1.0 + lax.erf(x * (0.5 ** 0.5)))


def _pack_bf16(x):
    return pltpu.bitcast(x.astype(BF16), jnp.uint32)


def _unpack_bf16(words):
    return pltpu.bitcast(words, BF16)


def _pack_kernel(x_ref, o_ref, *, transpose):
    x = x_ref[...]
    o_ref[...] = _pack_bf16(x.T if transpose else x)


def _pack_table(table, layer, transpose):
    _, n, d = table.shape
    rows = PEER_EXPERT_TILE
    if transpose:
        out_spec = pl.BlockSpec((d // 2, rows), lambda e: (0, e))
        out_shape = jax.ShapeDtypeStruct((d // 2, n), jnp.uint32)
    else:
        out_spec = pl.BlockSpec((rows // 2, d), lambda e: (e, 0))
        out_shape = jax.ShapeDtypeStruct((n // 2, d), jnp.uint32)
    return pl.pallas_call(
        functools.partial(_pack_kernel, transpose=transpose),
        grid=(n // rows,),
        in_specs=[pl.BlockSpec((None, rows, d), lambda e: (layer, e, 0))],
        out_specs=out_spec,
        out_shape=out_shape,
        compiler_params=_cparams("arbitrary"),
        name="pack_table_t" if transpose else "pack_table",
    )(table)


def _mod_kernel(c_ref, w_ref, b_ref, o_ref):
    c = c_ref[...]
    a = (c / (1.0 + jnp.exp(-c))).astype(BF16)
    o_ref[...] = jnp.dot(a, w_ref[...].astype(BF16), preferred_element_type=F32) + b_ref[...]


def _modulation(cvec, w_mod, b_mod):
    n = N_MOD * D_MODEL
    out = pl.pallas_call(
        _mod_kernel,
        grid=(DEPTH, n // MOD_COL_TILE),
        in_specs=[
            pl.BlockSpec((MOD_ROWS, D_MODEL), lambda l, j: (0, 0)),
            pl.BlockSpec((None, D_MODEL, MOD_COL_TILE), lambda l, j: (l, 0, j)),
            pl.BlockSpec((None, 1, MOD_COL_TILE), lambda l, j: (l, 0, j)),
        ],
        out_specs=pl.BlockSpec((None, MOD_ROWS, MOD_COL_TILE), lambda l, j: (l, 0, j)),
        out_shape=jax.ShapeDtypeStruct((DEPTH, MOD_ROWS, n), F32),
        compiler_params=_cparams("arbitrary", "arbitrary"),
        name="modulation",
    )(cvec, w_mod, b_mod.reshape(DEPTH, 1, n))
    return out.reshape(DEPTH, MOD_ROWS, N_MOD, D_MODEL)


_PROJ_SLICES = ((0, ATTN_W), (ATTN_W, KV_W), (ATTN_W + KV_W, KV_W),
                (ATTN_W + 2 * KV_W, SGU_W), (ATTN_W + 2 * KV_W + SGU_W, SGU_W),
                (ATTN_W + 2 * KV_W + 2 * SGU_W, FNET_W))


def _proj_kernel(x_ref, mod_ref, w_ref, *out_refs):
    h = _ln(x_ref[...]) * (1.0 + mod_ref[1:2, :]) + mod_ref[0:1, :]
    h = h.astype(BF16)
    for (start, width), o_ref in zip(_PROJ_SLICES, out_refs):
        o_ref[...] = jnp.dot(h, w_ref[:, start:start + width], preferred_element_type=F32)


def _in_projection(x, mod, mod_base, rows_per_mod, w_in):
    t = x.shape[0]
    tiles_per_mod = rows_per_mod // ROW_TILE
    row = lambda w: pl.BlockSpec((ROW_TILE, w), lambda i: (i, 0))
    return pl.pallas_call(
        _proj_kernel,
        grid=(t // ROW_TILE,),
        in_specs=[
            row(D_MODEL),
            pl.BlockSpec((None, N_MOD, D_MODEL), lambda i: (mod_base + i // tiles_per_mod, 0, 0)),
            pl.BlockSpec((D_MODEL, PROJ_W), lambda i: (0, 0)),
        ],
        out_specs=[row(w) for _, w in _PROJ_SLICES],
        out_shape=[jax.ShapeDtypeStruct((t, w), F32) for _, w in _PROJ_SLICES],
        compiler_params=_cparams("arbitrary"),
        name="in_projection",
    )(x, mod, w_in)


def _head_cols(g):
    return slice(g * HEAD_DIM, (g + 1) * HEAD_DIM)


def _stacked_sinks(sink_ref, rows):
    return jnp.concatenate([jnp.broadcast_to(sink_ref[g:g + 1, 0:1], (rows, 1)) for g in range(GROUP)],
                           axis=0)


def _ctx_attn_kernel(q_ref, k_ref, v_ref, sink_ref, o_ref):
    seq = q_ref.shape[0]
    k = k_ref[...].astype(BF16)
    v = v_ref[...].astype(BF16)
    q = jnp.concatenate([q_ref[:, _head_cols(g)] for g in range(GROUP)], axis=0).astype(BF16)
    s = lax.dot_general(q, k, _NT, preferred_element_type=F32) * (HEAD_DIM ** -0.5)
    sk = _stacked_sinks(sink_ref, seq)
    m = jnp.maximum(_row_reduce(jnp.maximum, jnp.max, s), sk)
    p = jnp.exp(s - m)
    den = _row_reduce(jnp.add, jnp.sum, p) + jnp.exp(sk - m)
    o = jnp.dot(p.astype(BF16), v, preferred_element_type=F32) / den
    for g in range(GROUP):
        o_ref[:, _head_cols(g)] = o[g * seq:(g + 1) * seq]


def _context_attention(q, k, v, sink_b, batch, seq):
    t = q.shape[0]
    qspec = pl.BlockSpec((seq, GROUP * HEAD_DIM), lambda b, h: (b, h))
    kvspec = pl.BlockSpec((seq, HEAD_DIM), lambda b, h: (b, h))
    return pl.pallas_call(
        _ctx_attn_kernel,
        grid=(batch, N_KV),
        in_specs=[qspec, kvspec, kvspec, pl.BlockSpec((None, SUBLANES, LANES), lambda b, h: (h, 0, 0))],
        out_specs=qspec,
        out_shape=jax.ShapeDtypeStruct((t, ATTN_W), F32),
        compiler_params=_cparams("arbitrary", "arbitrary"),
        name="context_attention",
    )(q, k, v, sink_b)


def _lat_attn_kernel(q_ref, k_ref, v_ref, ck_ref, cv_ref, cos_ref, sin_ref, sink_ref, o_ref,
                     kp_ref, vp_ref, *, seq):
    scale = HEAD_DIM ** -0.5
    lane = lax.broadcasted_iota(jnp.int32, (1, HEAD_DIM), 1)
    first = (lane % 64) < 32

    def rope(x, cos, sin):
        sw = jnp.where(first, pltpu.roll(x, 96, 1), pltpu.roll(x, 32, 1))
        return x * cos + sw * sin

    pad = jnp.zeros((BLOCK, HEAD_DIM), BF16)
    kp_ref[0:BLOCK, :] = pad
    kp_ref[seq + BLOCK:seq + 2 * BLOCK, :] = pad
    vp_ref[0:BLOCK, :] = pad
    vp_ref[seq + BLOCK:seq + 2 * BLOCK, :] = pad
    kp_ref[BLOCK:seq + BLOCK, :] = rope(k_ref[...], cos_ref[...], sin_ref[...]).astype(BF16)
    vp_ref[BLOCK:seq + BLOCK, :] = v_ref[...].astype(BF16)
    ck = ck_ref[...].astype(BF16)
    cv = cv_ref[...].astype(BF16)
    sk = _stacked_sinks(sink_ref, BLOCK)

    def body(qb, carry):
        r0 = pl.multiple_of(qb * BLOCK, BLOCK)
        kb = kp_ref[pl.ds(r0, 3 * BLOCK), :]
        vb = vp_ref[pl.ds(r0, 3 * BLOCK), :]
        cosq = cos_ref[pl.ds(r0, BLOCK), :]
        sinq = sin_ref[pl.ds(r0, BLOCK), :]
        r = lax.broadcasted_iota(jnp.int32, (GROUP * BLOCK, 3 * BLOCK), 0) & (BLOCK - 1)
        j = lax.broadcasted_iota(jnp.int32, (GROUP * BLOCK, 3 * BLOCK), 1)
        kpos = j + (r0 - BLOCK)
        valid = (j >= r) & (j <= r + 2 * WINDOW) & (kpos >= 0) & (kpos < seq)
        q = jnp.concatenate([rope(q_ref[pl.ds(r0, BLOCK), _head_cols(g)], cosq, sinq)
                             for g in range(GROUP)], axis=0).astype(BF16)
        sl = lax.dot_general(q, kb, _NT, preferred_element_type=F32) * scale
        sl = jnp.where(valid, sl, NEG_INF)
        sc = lax.dot_general(q, ck, _NT, preferred_element_type=F32) * scale
        m = jnp.maximum(_row_reduce(jnp.maximum, jnp.max, sl, sc), sk)
        p_l = jnp.exp(sl - m)
        p_c = jnp.exp(sc - m)
        den = _row_reduce(jnp.add, jnp.sum, p_l, p_c) + jnp.exp(sk - m)
        o = (jnp.dot(p_l.astype(BF16), vb, preferred_element_type=F32)
             + jnp.dot(p_c.astype(BF16), cv, preferred_element_type=F32)) / den
        for g in range(GROUP):
            o_ref[pl.ds(r0, BLOCK), _head_cols(g)] = o[g * BLOCK:(g + 1) * BLOCK]
        return carry

    lax.fori_loop(0, seq // BLOCK, body, 0, unroll=4)


def _latent_attention(q, k, v, cache_k, cache_v, layer, cos_t, sin_t, sink_b, batch, seq):
    t = q.shape[0]
    past = cache_k.shape[2]
    qspec = pl.BlockSpec((seq, GROUP * HEAD_DIM), lambda b, h: (b, h))
    kvspec = pl.BlockSpec((seq, HEAD_DIM), lambda b, h: (b, h))
    cspec = pl.BlockSpec((None, None, past, HEAD_DIM), lambda b, h: (b, layer, 0, h))
    tspec = pl.BlockSpec((seq, HEAD_DIM), lambda b, h: (0, 0))
    return pl.pallas_call(
        functools.partial(_lat_attn_kernel, seq=seq),
        grid=(batch, N_KV),
        in_specs=[qspec, kvspec, kvspec, cspec, cspec, tspec, tspec,
                  pl.BlockSpec((None, SUBLANES, LANES), lambda b, h: (h, 0, 0))],
        out_specs=qspec,
        out_shape=jax.ShapeDtypeStruct((t, ATTN_W), F32),
        scratch_shapes=[pltpu.VMEM((seq + 2 * BLOCK, HEAD_DIM), BF16),
                        pltpu.VMEM((seq + 2 * BLOCK, HEAD_DIM), BF16)],
        compiler_params=_cparams("arbitrary", "arbitrary"),
        name="latent_attention",
    )(q, k, v, cache_k, cache_v, cos_t, sin_t, sink_b)


def _rope_tables(seq):
    pos = jnp.arange(seq)
    r = (pos // GRID_W).astype(F32)
    col = (pos % GRID_W).astype(F32)
    n = HEAD_DIM // 4
    inv = ROPE_BASE ** (-jnp.arange(n, dtype=F32) / n)
    ang_r = r[:, None] * inv
    ang_c = col[:, None] * inv
    cos_t = jnp.concatenate([jnp.cos(ang_r), jnp.cos(ang_r), jnp.cos(ang_c), jnp.cos(ang_c)], axis=-1)
    sin_t = jnp.concatenate([-jnp.sin(ang_r), jnp.sin(ang_r), -jnp.sin(ang_c), jnp.sin(ang_c)], axis=-1)
    return cos_t, sin_t


def _sgu_kernel(ug_ref, vg_ref, g_ref, ws_ref, bias_ref, o_ref):
    for c in range(ROW_TILE // CHUNK):
        rows = slice(c * CHUNK, (c + 1) * CHUNK)
        for g in range(SGU_GROUPS):
            cols = slice(g * LANES, (g + 1) * LANES)
            u = _gelu(ug_ref[rows, cols])
            vn = _ln(_gelu(vg_ref[rows, cols])) * g_ref[0:1, cols]
            mixed = jnp.dot(ws_ref[g], vn.astype(BF16), preferred_element_type=F32) + bias_ref[:, cols]
            o_ref[rows, cols] = u * mixed


def _spatial_gating(ug, vg, sgu_g, sgu_w, sgu_bias):
    t = ug.shape[0]
    row = pl.BlockSpec((ROW_TILE, SGU_W), lambda i: (i, 0))
    return pl.pallas_call(
        _sgu_kernel,
        grid=(t // ROW_TILE,),
        in_specs=[row, row,
                  pl.BlockSpec((1, SGU_W), lambda i: (0, 0)),
                  pl.BlockSpec((SGU_GROUPS, CHUNK, CHUNK), lambda i: (0, 0, 0)),
                  pl.BlockSpec((CHUNK, SGU_W), lambda i: (0, 0))],
        out_specs=row,
        out_shape=jax.ShapeDtypeStruct((t, SGU_W), F32),
        compiler_params=_cparams("arbitrary"),
        name="spatial_gating",
    )(ug, vg, sgu_g, sgu_w, sgu_bias)


def _fnet_kernel(f_ref, cc_ref, cs_ref, o_ref, z_ref, *, seq):
    @pl.when(pl.program_id(1) == 0)
    def _():
        cc = cc_ref[...]
        for g in range(FNET_GROUPS):
            cols = slice(g * FNET_GD, (g + 1) * FNET_GD)
            zz = jnp.dot(f_ref[:, cols].astype(BF16), cc, preferred_element_type=F32)
            z_ref[0:seq, cols] = zz[:, :FNET_GD].astype(BF16)
            z_ref[seq:2 * seq, cols] = zz[:, FNET_GD:].astype(BF16)

    y = jnp.dot(cs_ref[...], z_ref[...], preferred_element_type=F32)
    o_ref[...] = y * ((seq * FNET_GD) ** -0.5)


def _dft_tables(n):
    k = jnp.arange(n, dtype=jnp.int32)
    ang = ((k[:, None] * k[None, :]) % n).astype(F32) * (2.0 * jnp.pi / n)
    return jnp.cos(ang), jnp.sin(ang)


def _fourier_mix(f, batch, seq):
    t = f.shape[0]
    tr = min(seq, 512)
    c_c, s_c = _dft_tables(FNET_GD)
    c_s, s_s = _dft_tables(seq)
    cc = jnp.concatenate([c_c, s_c], axis=1).astype(BF16)
    cs = jnp.concatenate([c_s, -s_s], axis=1).astype(BF16)
    nj = seq // tr
    return pl.pallas_call(
        functools.partial(_fnet_kernel, seq=seq),
        grid=(batch, nj),
        in_specs=[pl.BlockSpec((seq, FNET_W), lambda b, j: (b, 0)),
                  pl.BlockSpec((FNET_GD, 2 * FNET_GD), lambda b, j: (0, 0)),
                  pl.BlockSpec((tr, 2 * seq), lambda b, j: (j, 0))],
        out_specs=pl.BlockSpec((tr, FNET_W), lambda b, j: (b * nj + j, 0)),
        out_shape=jax.ShapeDtypeStruct((t, FNET_W), F32),
        scratch_shapes=[pltpu.VMEM((2 * seq, FNET_W), BF16)],
        compiler_params=_cparams("arbitrary", "arbitrary"),
        name="fourier_mix",
    )(f, cc, cs)


def _mix_kernel(x_ref, a_ref, s_ref, f_ref, wo_ref, mod_ref, g1_ref, b1_ref, wq_ref,
                x1_ref, h2_ref, qp_ref):
    mix = jnp.dot(a_ref[...].astype(BF16), wo_ref[0:ATTN_W, :], preferred_element_type=F32)
    mix += jnp.dot(s_ref[...].astype(BF16), wo_ref[ATTN_W:ATTN_W + SGU_W, :], preferred_element_type=F32)
    mix += jnp.dot(f_ref[...].astype(BF16), wo_ref[ATTN_W + SGU_W:, :], preferred_element_type=F32)
    x1 = _ln(DN_ALPHA * x_ref[...] + mod_ref[2:3, :] * mix) * g1_ref[...] + b1_ref[...]
    x1_ref[...] = x1
    h2 = _ln(x1) * (1.0 + mod_ref[4:5, :]) + mod_ref[3:4, :]
    h2_ref[...] = _pack_bf16(h2.T)
    q = jnp.dot(h2.astype(BF16), wq_ref[...], preferred_element_type=F32)
    for h in range(PEER_HEADS):
        qp_ref[h] = q[:, h * PEER_QDIM:(h + 1) * PEER_QDIM].astype(BF16)


def _out_projection(x, attn, sg, fn, w_out, mod, mod_base, rows_per_mod, ln_g, ln_b, wq):
    t = x.shape[0]
    tiles_per_mod = rows_per_mod // ROW_TILE
    row = lambda w: pl.BlockSpec((ROW_TILE, w), lambda i: (i, 0))
    full = lambda a, b: pl.BlockSpec((a, b), lambda i: (0, 0))
    return pl.pallas_call(
        _mix_kernel,
        grid=(t // ROW_TILE,),
        in_specs=[row(D_MODEL), row(ATTN_W), row(SGU_W), row(FNET_W),
                  full(D_MODEL, D_MODEL),
                  pl.BlockSpec((None, N_MOD, D_MODEL), lambda i: (mod_base + i // tiles_per_mod, 0, 0)),
                  full(1, D_MODEL), full(1, D_MODEL),
                  full(D_MODEL, PEER_HEADS * PEER_QDIM)],
        out_specs=[row(D_MODEL), pl.BlockSpec((D_MODEL // 2, ROW_TILE), lambda i: (0, i)),
                   pl.BlockSpec((PEER_HEADS, ROW_TILE, PEER_QDIM), lambda i: (0, i, 0))],
        out_shape=[jax.ShapeDtypeStruct((t, D_MODEL), F32),
                   jax.ShapeDtypeStruct((D_MODEL // 2, t), jnp.uint32),
                   jax.ShapeDtypeStruct((PEER_HEADS, t, PEER_QDIM), BF16)],
        compiler_params=_cparams("arbitrary"),
        name="out_projection",
    )(x, attn, sg, fn, w_out, mod, ln_g, ln_b, wq)


def _pair_candidates(v1, v2):
    half = PEER_TOPK // 2
    pieces = [v1[0:1] + v2] + [v1[r1:r1 + 1] + v2[0:half] for r1 in range(1, PEER_TOPK)]
    pos_pieces = [lax.broadcasted_iota(jnp.int32, (PEER_TOPK, LANES), 0)]
    for r1 in range(1, PEER_TOPK):
        pos_pieces.append(lax.broadcasted_iota(jnp.int32, (half, LANES), 0) + r1 * PEER_TOPK)
    spans = [(0, PEER_TOPK)] + [(PEER_TOPK + (r1 - 1) * half, half) for r1 in range(1, PEER_TOPK)]
    return jnp.concatenate(pieces, axis=0), jnp.concatenate(pos_pieces, axis=0).astype(F32), spans


def _descending_maxima(slabs):
    w = slabs[0].shape[1]
    slot = lax.broadcasted_iota(jnp.int32, (PEER_TOPK, w), 0)

    def body(k, carry):
        out = []
        for s, (prev, vals) in zip(slabs, carry):
            m = jnp.max(jnp.where(s < prev, s, -jnp.inf), axis=0, keepdims=True)
            out.append((m, jnp.where(slot == k, m, vals)))
        return tuple(out)

    init = tuple((jnp.full((1, w), jnp.inf, F32), jnp.zeros((PEER_TOPK, w), F32)) for _ in slabs)
    return [vals for _, vals in lax.fori_loop(0, PEER_TOPK, body, init)]


def _count_rows(mask):
    return jnp.sum(jnp.where(mask, 1.0, 0.0), axis=0, keepdims=True)


def _topk_tables_distinct(s1, s2):
    nc = len(s1)
    k = float(PEER_TOPK)
    vals = _descending_maxima(s1 + s2)
    v1, v2 = vals[:nc], vals[nc:]
    cands = [_pair_candidates(a, b) for a, b in zip(v1, v2)]
    best = _descending_maxima([cand for cand, _, _ in cands])
    tables, bad = [], jnp.zeros((1, LANES), F32)
    for c in range(nc):
        cand, _, spans = cands[c]
        taken = jnp.where(cand >= best[c][PEER_TOPK - 1:PEER_TOPK], 1.0, 0.0)
        cnt = [jnp.sum(taken[lo:lo + n], axis=0, keepdims=True) for lo, n in spans]
        z = jnp.sum(jnp.exp(best[c] - best[c][0:1]), axis=0, keepdims=True)
        rank2 = jnp.zeros((PEER_NKEYS, LANES), F32)
        n1 = jnp.zeros((PEER_NKEYS, LANES), F32)
        for r in range(PEER_TOPK):
            rank2 = rank2 + jnp.where(v2[c][r:r + 1] > s2[c], 1.0, 0.0)
            n1 = jnp.where(s1[c] == v1[c][r:r + 1], cnt[r], n1)
        tables.append((rank2, jnp.exp(s2[c] - v2[c][0:1]), n1, jnp.exp(s1[c] - v1[c][0:1]) / z))
        for s, v in ((s1[c], v1[c]), (s2[c], v2[c])):
            bad = bad + jnp.abs(_count_rows(s >= v[PEER_TOPK - 1:PEER_TOPK]) - k)
        bad = bad + jnp.abs(sum(cnt) - k)
    return tables, bad


def _top16_rows(scores):
    n, w = scores[0].shape
    row = lax.broadcasted_iota(jnp.int32, (n, w), 0).astype(F32)
    slot = lax.broadcasted_iota(jnp.int32, (PEER_TOPK, w), 0)

    def body(k, carry):
        kf = lax.convert_element_type(k, F32)
        out = []
        for cur, rank, vals in carry:
            m = jnp.max(cur, axis=0, keepdims=True)
            hit = row == jnp.min(jnp.where(cur == m, row, float(n)), axis=0, keepdims=True)
            out.append((jnp.where(hit, -jnp.inf, cur), jnp.where(hit, kf, rank),
                        jnp.where(slot == k, m, vals)))
        return tuple(out)

    init = tuple((s, jnp.full((n, w), float(PEER_TOPK), F32), jnp.zeros((PEER_TOPK, w), F32))
                 for s in scores)
    res = lax.fori_loop(0, PEER_TOPK, body, init)
    return [(rank, vals) for _, rank, vals in res]


def _topk_tables_exact(s1, s2):
    nc = len(s1)
    tops = _top16_rows(s1 + s2)
    slot = lax.broadcasted_iota(jnp.int32, (PEER_TOPK, LANES), 0).astype(F32)
    cands = [_pair_candidates(tops[c][1], tops[nc + c][1]) for c in range(nc)]

    def body(k, carry):
        out = []
        for (cand, cnt, z), (cand0, pos, _) in zip(carry, cands):
            m = jnp.max(cand, axis=0, keepdims=True)
            p = jnp.min(jnp.where(cand == m, pos, 4096.0), axis=0, keepdims=True)
            out.append((jnp.where(pos == p, -jnp.inf, cand),
                        cnt + jnp.where(slot == jnp.floor(p * (1.0 / PEER_TOPK)), 1.0, 0.0),
                        z + jnp.exp(m - cand0[0:1])))
        return tuple(out)

    init = tuple((cand, jnp.zeros((PEER_TOPK, LANES), F32), jnp.zeros((1, LANES), F32))
                 for cand, _, _ in cands)
    res = lax.fori_loop(0, PEER_TOPK, body, init)
    tables = []
    for c in range(nc):
        (rank1, v1), (rank2, v2) = tops[c], tops[nc + c]
        _, cnt, z = res[c]
        n1 = jnp.zeros((PEER_NKEYS, LANES), F32)
        for r1 in range(PEER_TOPK):
            n1 = jnp.where(rank1 == float(r1), cnt[r1:r1 + 1], n1)
        tables.append((rank2, jnp.exp(s2[c] - v2[0:1]), n1, jnp.exp(s1[c] - v1[0:1]) / z))
    return tables


def _topk_kernel(q_ref, km_ref, r2_ref, e2_ref, n1_ref, e1_ref, *, tm):
    km = km_ref[...]
    toks = [slice(c * LANES, (c + 1) * LANES) for c in range(tm // LANES)]
    sts = [lax.dot_general(km, q_ref[tok, :], _NT, preferred_element_type=F32) for tok in toks]
    s1 = [st[0:PEER_NKEYS] for st in sts]
    s2 = [st[PEER_NKEYS:2 * PEER_NKEYS] for st in sts]

    def store(tables):
        for tok, (rank2, e2, n1, e1) in zip(toks, tables):
            r2_ref[:, tok] = _pack_bf16(rank2)
            e2_ref[:, tok] = _pack_bf16(e2)
            n1_ref[:, tok] = n1
            e1_ref[:, tok] = e1

    tables, bad = _topk_tables_distinct(s1, s2)
    store(tables)

    @pl.when(jnp.max(bad) > 0.0)
    def _():
        store(_topk_tables_exact(s1, s2))


def _peer_topk(qp, kmat):
    t = qp.shape[1]
    tm = min(TOPK_TOKEN_TILE, t)
    slab_spec = pl.BlockSpec((PEER_NKEYS // 2, tm), lambda i, h: (h, i))
    slab_shape = jax.ShapeDtypeStruct((PEER_HEADS * PEER_NKEYS // 2, t), jnp.uint32)
    row_spec = pl.BlockSpec((PEER_NKEYS, tm), lambda i, h: (h, i))
    row_shape = jax.ShapeDtypeStruct((PEER_HEADS * PEER_NKEYS, t), F32)
    return pl.pallas_call(
        functools.partial(_topk_kernel, tm=tm),
        grid=(t // tm, PEER_HEADS),
        in_specs=[pl.BlockSpec((None, tm, PEER_QDIM), lambda i, h: (h, i, 0)),
                  pl.BlockSpec((None, 2 * PEER_NKEYS, PEER_QDIM), lambda i, h: (h, 0, 0))],
        out_specs=[slab_spec, slab_spec, row_spec, row_spec],
        out_shape=[slab_shape, slab_shape, row_shape, row_shape],
        compiler_params=_cparams("arbitrary", "arbitrary"),
        name="peer_topk",
    )(qp, kmat)


def _peer_kernel(h_ref, u_ref, vt_ref, r2_ref, e2_ref, n1_ref, e1_ref, o_ref, w_ref, *, tm):
    e = pl.program_id(1)
    words = PEER_NKEYS // 2

    @pl.when(e == 0)
    def _():
        o_ref[...] = jnp.zeros_like(o_ref)

    ht = _unpack_bf16(h_ref[...])
    for sb in range(PEER_EXPERT_TILE // PEER_EXPERT_SUBTILE):
        urows = slice(sb * PEER_EXPERT_SUBTILE // 2, (sb + 1) * PEER_EXPERT_SUBTILE // 2)
        at = jnp.dot(_unpack_bf16(u_ref[urows, :]), ht, preferred_element_type=F32)
        for bs in range(PEER_EXPERT_SUBTILE // PEER_NKEYS):
            bi = sb * (PEER_EXPERT_SUBTILE // PEER_NKEYS) + bs
            rows = slice(bi * words, (bi + 1) * words)
            for c in range(tm // LANES):
                tok = slice(c * LANES, (c + 1) * LANES)
                gate = jnp.zeros((PEER_NKEYS, LANES), BF16)
                for h in range(PEER_HEADS):
                    keys = slice(h * words, (h + 1) * words)
                    grp = pl.ds(pl.multiple_of(h * PEER_NKEYS + e * SUBLANES, SUBLANES), SUBLANES)
                    n1 = jnp.broadcast_to(n1_ref[grp, tok][bi:bi + 1, :], (PEER_NKEYS, LANES)).astype(BF16)
                    e1 = jnp.broadcast_to(e1_ref[grp, tok][bi:bi + 1, :], (PEER_NKEYS, LANES)).astype(BF16)
                    r2 = _unpack_bf16(r2_ref[keys, tok])
                    e2 = _unpack_bf16(e2_ref[keys, tok])
                    gate += jnp.where(r2 < n1, e2, 0.0) * e1
                act = _gelu(at[bs * PEER_NKEYS:(bs + 1) * PEER_NKEYS, tok]).astype(BF16)
                w_ref[rows, tok] = pltpu.bitcast(act * gate, jnp.uint32)
    o_ref[...] += jnp.dot(_unpack_bf16(vt_ref[...]), _unpack_bf16(w_ref[...]), preferred_element_type=F32)


def _peer_apply(h2t, u, vt, tables):
    t = h2t.shape[1]
    tm = min(PEER_TOKEN_TILE, t)
    eb = PEER_EXPERT_TILE
    slab = pl.BlockSpec((PEER_HEADS * PEER_NKEYS // 2, tm), lambda i, e: (0, i))
    rowt = pl.BlockSpec((PEER_HEADS * PEER_NKEYS, tm), lambda i, e: (0, i))
    return pl.pallas_call(
        functools.partial(_peer_kernel, tm=tm),
        grid=(t // tm, PEER_N // eb),
        in_specs=[pl.BlockSpec((D_MODEL // 2, tm), lambda i, e: (0, i)),
                  pl.BlockSpec((eb // 2, D_MODEL), lambda i, e: (e, 0)),
                  pl.BlockSpec((D_MODEL // 2, eb), lambda i, e: (0, e)),
                  slab, slab, rowt, rowt],
        out_specs=pl.BlockSpec((D_MODEL, tm), lambda i, e: (0, i)),
        out_shape=jax.ShapeDtypeStruct((D_MODEL, t), F32),
        scratch_shapes=[pltpu.VMEM((eb // 2, tm), jnp.uint32)],
        compiler_params=_cparams("arbitrary", "arbitrary"),
        name="peer_apply",
    )(h2t, u, vt, *tables)


def _final_kernel(x_ref, pt_ref, mod_ref, g_ref, b_ref, o_ref):
    y = DN_ALPHA * x_ref[...] + mod_ref[5:6, :] * pt_ref[...].T
    o_ref[...] = _ln(y) * g_ref[...] + b_ref[...]


def _finalize(x1, peer_t, mod, mod_base, rows_per_mod, ln_g, ln_b):
    t = x1.shape[0]
    tiles_per_mod = rows_per_mod // ROW_TILE
    row = pl.BlockSpec((ROW_TILE, D_MODEL), lambda i: (i, 0))
    vec = pl.BlockSpec((1, D_MODEL), lambda i: (0, 0))
    return pl.pallas_call(
        _final_kernel,
        grid=(t // ROW_TILE,),
        in_specs=[row,
                  pl.BlockSpec((D_MODEL, ROW_TILE), lambda i: (0, i)),
                  pl.BlockSpec((None, N_MOD, D_MODEL), lambda i: (mod_base + i // tiles_per_mod, 0, 0)),
                  vec, vec],
        out_specs=row,
        out_shape=jax.ShapeDtypeStruct((t, D_MODEL), F32),
        compiler_params=_cparams("arbitrary"),
        name="finalize",
    )(x1, peer_t, mod, ln_g, ln_b)


def _prep_layer(l, w_in, attn_sink, sgu_g, sgu_w, sgu_b, w_out, ln1_g, ln1_b,
                peer_wq, peer_keys, peer_u, peer_v, ln2_g, ln2_b):
    keys = peer_keys[l]
    zeros = jnp.zeros_like(keys[:, 0])
    kmat = jnp.concatenate([jnp.concatenate([keys[:, 0], zeros], axis=-1),
                            jnp.concatenate([zeros, keys[:, 1]], axis=-1)], axis=1)
    return dict(
        w_in=w_in[l].astype(BF16),
        sink_b=jnp.broadcast_to(
            jnp.pad(attn_sink[l].reshape(N_KV, GROUP), ((0, 0), (0, SUBLANES - GROUP)))[:, :, None],
            (N_KV, SUBLANES, LANES)),
        sgu_g=sgu_g[l].reshape(1, SGU_W),
        sgu_w=sgu_w[l].astype(BF16),
        sgu_bias=jnp.repeat(sgu_b[l].T, LANES, axis=1),
        w_out=w_out[l].astype(BF16),
        ln1_g=ln1_g[l].reshape(1, D_MODEL), ln1_b=ln1_b[l].reshape(1, D_MODEL),
        wq=peer_wq[l].astype(BF16),
        kmat=kmat.astype(BF16),
        u=_pack_table(peer_u, l, transpose=False),
        vt=_pack_table(peer_v, l, transpose=True),
        ln2_g=ln2_g[l].reshape(1, D_MODEL), ln2_b=ln2_b[l].reshape(1, D_MODEL),
    )


def _layer(x, batch, seq, mod, mod_base, rows_per_mod, lw, ctx):
    q, k, v, ug, vg, f = _in_projection(x, mod, mod_base, rows_per_mod, lw["w_in"])
    if ctx is None:
        attn = _context_attention(q, k, v, lw["sink_b"], batch, seq)
    else:
        cache_k, cache_v, layer, cos_t, sin_t = ctx
        attn = _latent_attention(q, k, v, cache_k, cache_v, layer, cos_t, sin_t, lw["sink_b"], batch, seq)
    sg = _spatial_gating(ug, vg, lw["sgu_g"], lw["sgu_w"], lw["sgu_bias"])
    fn = _fourier_mix(f, batch, seq)
    x1, h2, qp = _out_projection(x, attn, sg, fn, lw["w_out"], mod, mod_base, rows_per_mod,
                                 lw["ln1_g"], lw["ln1_b"], lw["wq"])
    tables = _peer_topk(qp, lw["kmat"])
    peer_t = _peer_apply(h2, lw["u"], lw["vt"], tables)
    x2 = _finalize(x1, peer_t, mod, mod_base, rows_per_mod, lw["ln2_g"], lw["ln2_b"])
    return x2, k, v


def kernel(x_prompt, x_sample, cache_k, cache_v, c, c_ctx, w_mod, b_mod, w_in, attn_sink, sgu_g, sgu_w, sgu_b, w_out, ln1_g, ln1_b, peer_wq, peer_keys, peer_u, peer_v, ln2_g, ln2_b):
    batch, seq, _ = x_prompt.shape
    dec_batch, dec_seq, _ = x_sample.shape
    past = cache_k.shape[2]
    assert 1 + dec_batch <= MOD_ROWS

    cvec = jnp.concatenate([c_ctx[None, :], c, jnp.zeros((MOD_ROWS - 1 - dec_batch, D_MODEL), F32)], axis=0)
    mod = _modulation(cvec, w_mod, b_mod)
    cos_t, sin_t = _rope_tables(dec_seq)
    ck = cache_k.reshape(dec_batch, DEPTH, past, KV_W)
    cv = cache_v.reshape(dec_batch, DEPTH, past, KV_W)

    xp = x_prompt.reshape(batch * seq, D_MODEL)
    xs = x_sample.reshape(dec_batch * dec_seq, D_MODEL)
    new_k, new_v = [], []
    for l in range(DEPTH):
        lw = _prep_layer(l, w_in, attn_sink, sgu_g, sgu_w, sgu_b, w_out, ln1_g, ln1_b,
                         peer_wq, peer_keys, peer_u, peer_v, ln2_g, ln2_b)
        xp, kl, vl = _layer(xp, batch, seq, mod[l], 0, batch * seq, lw, None)
        new_k.append(kl.reshape(batch, seq, N_KV, HEAD_DIM))
        new_v.append(vl.reshape(batch, seq, N_KV, HEAD_DIM))
        xs, _, _ = _layer(xs, dec_batch, dec_seq, mod[l], 1, dec_seq, lw, (ck, cv, l, cos_t, sin_t))
    return (xp.reshape(batch, seq, D_MODEL), xs.reshape(dec_batch, dec_seq, D_MODEL),
            jnp.stack(new_k, axis=1), jnp.stack(new_v, axis=1))
```

```python
import functools

import jax
import jax.numpy as jnp
from jax import lax
from jax.experimental import pallas as pl
from jax.experimental.pallas import tpu as pltpu

D_MODEL = 2048
DEPTH = 2
GRID_W = 64
HEAD_DIM = 128
N_HEADS = 8
N_KV = 2
GROUP = N_HEADS // N_KV
ATTN_W = N_HEADS * HEAD_DIM
KV_W = N_KV * HEAD_DIM
SGU_GROUPS = 4
SGU_W = 512
FNET_GROUPS = 4
FNET_W = 512
FNET_GD = 128
PROJ_W = ATTN_W + 2 * KV_W + 2 * SGU_W + FNET_W
BLOCK = 128
WINDOW = 128
CHUNK = 128
ROPE_BASE = 10000.0
N_MOD = 6
PEER_HEADS = 8
PEER_NKEYS = 128
PEER_N = PEER_NKEYS * PEER_NKEYS
PEER_TOPK = 16
PEER_QDIM = 128
PEER_HALF = PEER_QDIM // 2
DN_ALPHA = (2 * DEPTH) ** 0.25
LN_EPS = 1e-5
NEG_INF = -1e30

F32 = jnp.float32
BF16 = jnp.bfloat16
FP8 = jnp.float8_e4m3fn
FP8_TOP = 256.0
LANES = 128
SUBLANES = 8
V7X_VMEM_BYTES = 64 * 2 ** 20
VMEM_LIMIT = V7X_VMEM_BYTES - 8 * 2 ** 20
ROW_TILE = 256
TOPK_TOKEN_TILE = 512
PEER_TOKEN_TILE = 512
PEER_EXPERT_SUBTILE = 256
PEER_EXPERT_TILE = SUBLANES * PEER_NKEYS
MOD_COL_TILE = 1024
MOD_ROWS = 16

_NT = (((1,), (1,)), ((), ()))


def _cparams(*sem):
    return pltpu.CompilerParams(dimension_semantics=sem, vmem_limit_bytes=VMEM_LIMIT)


def _row_reduce(op, lane_op, *blocks):
    acc = None
    for x in blocks:
        for i in range(x.shape[1] // LANES):
            chunk = x[:, i * LANES:(i + 1) * LANES]
            acc = chunk if acc is None else op(acc, chunk)
    return lane_op(acc, axis=-1, keepdims=True)


def _ln(x):
    inv_n = 1.0 / x.shape[-1]
    xc = x - _row_reduce(jnp.add, jnp.sum, x) * inv_n
    var = _row_reduce(jnp.add, jnp.sum, xc * xc) * inv_n
    return xc * lax.rsqrt(var + LN_EPS)


def _gelu(x):
    return 0.5 * x * (1.0 + lax.erf(x * (0.5 ** 0.5)))


def _pack_bf16(x):
    return pltpu.bitcast(x.astype(BF16), jnp.uint32)


def _unpack_bf16(words):
    return pltpu.bitcast(words, BF16)


def _fp8_scale(amax):
    return jnp.where(amax > 0.0, amax, FP8_TOP) * (1.0 / FP8_TOP)


def _pack_fp8(x):
    return pltpu.bitcast(x.astype(FP8), jnp.uint32)


def _unpack_fp8(words):
    return pltpu.bitcast(words, FP8)


def _pack_fp8_cols(xt):
    scale = _fp8_scale(jnp.max(jnp.abs(xt), axis=0, keepdims=True))
    return _pack_fp8(xt / scale), scale


def _pack_rows_kernel(x_ref, o_ref, s_ref):
    x = x_ref[...]
    scale = _fp8_scale(_row_reduce(jnp.maximum, jnp.max, jnp.abs(x)))
    o_ref[...] = _pack_fp8(x / scale)
    s_ref[...] = jnp.broadcast_to(scale, s_ref.shape)


def _pack_transposed_kernel(x_ref, o_ref):
    o_ref[...] = _pack_bf16(x_ref[...].T)


def _pack_table_fp8(table, layer):
    _, n, d = table.shape
    rows = PEER_EXPERT_TILE
    return pl.pallas_call(
        _pack_rows_kernel,
        grid=(n // rows,),
        in_specs=[pl.BlockSpec((None, rows, d), lambda e: (layer, e, 0))],
        out_specs=[pl.BlockSpec((rows // 4, d), lambda e: (e, 0)),
                   pl.BlockSpec((rows, LANES), lambda e: (e, 0))],
        out_shape=[jax.ShapeDtypeStruct((n // 4, d), jnp.uint32),
                   jax.ShapeDtypeStruct((n, LANES), F32)],
        compiler_params=_cparams("arbitrary"),
        name="pack_table_fp8",
    )(table)


def _pack_table_transposed(table, layer):
    _, n, d = table.shape
    rows = PEER_EXPERT_TILE
    return pl.pallas_call(
        _pack_transposed_kernel,
        grid=(n // rows,),
        in_specs=[pl.BlockSpec((None, rows, d), lambda e: (layer, e, 0))],
        out_specs=pl.BlockSpec((d // 2, rows), lambda e: (0, e)),
        out_shape=jax.ShapeDtypeStruct((d // 2, n), jnp.uint32),
        compiler_params=_cparams("arbitrary"),
        name="pack_table_t",
    )(table)


def _mod_kernel(c_ref, w_ref, b_ref, o_ref):
    c = c_ref[...]
    a = (c / (1.0 + jnp.exp(-c))).astype(BF16)
    o_ref[...] = jnp.dot(a, w_ref[...].astype(BF16), preferred_element_type=F32) + b_ref[...]


def _modulation(cvec, w_mod, b_mod):
    n = N_MOD * D_MODEL
    out = pl.pallas_call(
        _mod_kernel,
        grid=(DEPTH, n // MOD_COL_TILE),
        in_specs=[
            pl.BlockSpec((MOD_ROWS, D_MODEL), lambda l, j: (0, 0)),
            pl.BlockSpec((None, D_MODEL, MOD_COL_TILE), lambda l, j: (l, 0, j)),
            pl.BlockSpec((None, 1, MOD_COL_TILE), lambda l, j: (l, 0, j)),
        ],
        out_specs=pl.BlockSpec((None, MOD_ROWS, MOD_COL_TILE), lambda l, j: (l, 0, j)),
        out_shape=jax.ShapeDtypeStruct((DEPTH, MOD_ROWS, n), F32),
        compiler_params=_cparams("arbitrary", "arbitrary"),
        name="modulation",
    )(cvec, w_mod, b_mod.reshape(DEPTH, 1, n))
    return out.reshape(DEPTH, MOD_ROWS, N_MOD, D_MODEL)


_PROJ_SLICES = ((0, ATTN_W), (ATTN_W, KV_W), (ATTN_W + KV_W, KV_W),
                (ATTN_W + 2 * KV_W, SGU_W), (ATTN_W + 2 * KV_W + SGU_W, SGU_W),
                (ATTN_W + 2 * KV_W + 2 * SGU_W, FNET_W))


def _proj_kernel(x_ref, mod_ref, w_ref, *out_refs):
    h = _ln(x_ref[...]) * (1.0 + mod_ref[1:2, :]) + mod_ref[0:1, :]
    h = h.astype(BF16)
    for (start, width), o_ref in zip(_PROJ_SLICES, out_refs):
        o_ref[...] = jnp.dot(h, w_ref[:, start:start + width], preferred_element_type=F32)


def _in_projection(x, mod, mod_base, rows_per_mod, w_in):
    t = x.shape[0]
    tiles_per_mod = rows_per_mod // ROW_TILE
    row = lambda w: pl.BlockSpec((ROW_TILE, w), lambda i: (i, 0))
    return pl.pallas_call(
        _proj_kernel,
        grid=(t // ROW_TILE,),
        in_specs=[
            row(D_MODEL),
            pl.BlockSpec((None, N_MOD, D_MODEL), lambda i: (mod_base + i // tiles_per_mod, 0, 0)),
            pl.BlockSpec((D_MODEL, PROJ_W), lambda i: (0, 0)),
        ],
        out_specs=[row(w) for _, w in _PROJ_SLICES],
        out_shape=[jax.ShapeDtypeStruct((t, w), F32) for _, w in _PROJ_SLICES],
        compiler_params=_cparams("arbitrary"),
        name="in_projection",
    )(x, mod, w_in)


def _head_cols(g):
    return slice(g * HEAD_DIM, (g + 1) * HEAD_DIM)


def _stacked_sinks(sink_ref, rows):
    return jnp.concatenate([jnp.broadcast_to(sink_ref[g:g + 1, 0:1], (rows, 1)) for g in range(GROUP)],
                           axis=0)


def _ctx_attn_kernel(q_ref, k_ref, v_ref, sink_ref, o_ref):
    seq = q_ref.shape[0]
    k = k_ref[...].astype(BF16)
    v = v_ref[...].astype(BF16)
    q = jnp.concatenate([q_ref[:, _head_cols(g)] for g in range(GROUP)], axis=0).astype(BF16)
    s = lax.dot_general(q, k, _NT, preferred_element_type=F32) * (HEAD_DIM ** -0.5)
    sk = _stacked_sinks(sink_ref, seq)
    m = jnp.maximum(_row_reduce(jnp.maximum, jnp.max, s), sk)
    p = jnp.exp(s - m)
    den = _row_reduce(jnp.add, jnp.sum, p) + jnp.exp(sk - m)
    o = jnp.dot(p.astype(BF16), v, preferred_element_type=F32) / den
    for g in range(GROUP):
        o_ref[:, _head_cols(g)] = o[g * seq:(g + 1) * seq]


def _context_attention(q, k, v, sink_b, batch, seq):
    t = q.shape[0]
    qspec = pl.BlockSpec((seq, GROUP * HEAD_DIM), lambda b, h: (b, h))
    kvspec = pl.BlockSpec((seq, HEAD_DIM), lambda b, h: (b, h))
    return pl.pallas_call(
        _ctx_attn_kernel,
        grid=(batch, N_KV),
        in_specs=[qspec, kvspec, kvspec, pl.BlockSpec((None, SUBLANES, LANES), lambda b, h: (h, 0, 0))],
        out_specs=qspec,
        out_shape=jax.ShapeDtypeStruct((t, ATTN_W), F32),
        compiler_params=_cparams("arbitrary", "arbitrary"),
        name="context_attention",
    )(q, k, v, sink_b)


def _lat_attn_kernel(q_ref, k_ref, v_ref, ck_ref, cv_ref, cos_ref, sin_ref, sink_ref, o_ref,
                     kp_ref, vp_ref, *, seq):
    scale = HEAD_DIM ** -0.5
    lane = lax.broadcasted_iota(jnp.int32, (1, HEAD_DIM), 1)
    first = (lane % 64) < 32

    def rope(x, cos, sin):
        sw = jnp.where(first, pltpu.roll(x, 96, 1), pltpu.roll(x, 32, 1))
        return x * cos + sw * sin

    pad = jnp.zeros((BLOCK, HEAD_DIM), BF16)
    kp_ref[0:BLOCK, :] = pad
    kp_ref[seq + BLOCK:seq + 2 * BLOCK, :] = pad
    vp_ref[0:BLOCK, :] = pad
    vp_ref[seq + BLOCK:seq + 2 * BLOCK, :] = pad
    kp_ref[BLOCK:seq + BLOCK, :] = rope(k_ref[...], cos_ref[...], sin_ref[...]).astype(BF16)
    vp_ref[BLOCK:seq + BLOCK, :] = v_ref[...].astype(BF16)
    ck = ck_ref[...].astype(BF16)
    cv = cv_ref[...].astype(BF16)
    sk = _stacked_sinks(sink_ref, BLOCK)

    def body(qb, carry):
        r0 = pl.multiple_of(qb * BLOCK, BLOCK)
        kb = kp_ref[pl.ds(r0, 3 * BLOCK), :]
        vb = vp_ref[pl.ds(r0, 3 * BLOCK), :]
        cosq = cos_ref[pl.ds(r0, BLOCK), :]
        sinq = sin_ref[pl.ds(r0, BLOCK), :]
        r = lax.broadcasted_iota(jnp.int32, (GROUP * BLOCK, 3 * BLOCK), 0) & (BLOCK - 1)
        j = lax.broadcasted_iota(jnp.int32, (GROUP * BLOCK, 3 * BLOCK), 1)
        kpos = j + (r0 - BLOCK)
        valid = (j >= r) & (j <= r + 2 * WINDOW) & (kpos >= 0) & (kpos < seq)
        q = jnp.concatenate([rope(q_ref[pl.ds(r0, BLOCK), _head_cols(g)], cosq, sinq)
                             for g in range(GROUP)], axis=0).astype(BF16)
        sl = lax.dot_general(q, kb, _NT, preferred_element_type=F32) * scale
        sl = jnp.where(valid, sl, NEG_INF)
        sc = lax.dot_general(q, ck, _NT, preferred_element_type=F32) * scale
        m = jnp.maximum(_row_reduce(jnp.maximum, jnp.max, sl, sc), sk)
        p_l = jnp.exp(sl - m)
        p_c = jnp.exp(sc - m)
        den = _row_reduce(jnp.add, jnp.sum, p_l, p_c) + jnp.exp(sk - m)
        o = (jnp.dot(p_l.astype(BF16), vb, preferred_element_type=F32)
             + jnp.dot(p_c.astype(BF16), cv, preferred_element_type=F32)) / den
        for g in range(GROUP):
            o_ref[pl.ds(r0, BLOCK), _head_cols(g)] = o[g * BLOCK:(g + 1) * BLOCK]
        return carry

    lax.fori_loop(0, seq // BLOCK, body, 0, unroll=4)


def _latent_attention(q, k, v, cache_k, cache_v, layer, cos_t, sin_t, sink_b, batch, seq):
    t = q.shape[0]
    past = cache_k.shape[2]
    qspec = pl.BlockSpec((seq, GROUP * HEAD_DIM), lambda b, h: (b, h))
    kvspec = pl.BlockSpec((seq, HEAD_DIM), lambda b, h: (b, h))
    cspec = pl.BlockSpec((None, None, past, HEAD_DIM), lambda b, h: (b, layer, 0, h))
    tspec = pl.BlockSpec((seq, HEAD_DIM), lambda b, h: (0, 0))
    return pl.pallas_call(
        functools.partial(_lat_attn_kernel, seq=seq),
        grid=(batch, N_KV),
        in_specs=[qspec, kvspec, kvspec, cspec, cspec, tspec, tspec,
                  pl.BlockSpec((None, SUBLANES, LANES), lambda b, h: (h, 0, 0))],
        out_specs=qspec,
        out_shape=jax.ShapeDtypeStruct((t, ATTN_W), F32),
        scratch_shapes=[pltpu.VMEM((seq + 2 * BLOCK, HEAD_DIM), BF16),
                        pltpu.VMEM((seq + 2 * BLOCK, HEAD_DIM), BF16)],
        compiler_params=_cparams("arbitrary", "arbitrary"),
        name="latent_attention",
    )(q, k, v, cache_k, cache_v, cos_t, sin_t, sink_b)


def _rope_tables(seq):
    pos = jnp.arange(seq)
    r = (pos // GRID_W).astype(F32)
    col = (pos % GRID_W).astype(F32)
    n = HEAD_DIM // 4
    inv = ROPE_BASE ** (-jnp.arange(n, dtype=F32) / n)
    ang_r = r[:, None] * inv
    ang_c = col[:, None] * inv
    cos_t = jnp.concatenate([jnp.cos(ang_r), jnp.cos(ang_r), jnp.cos(ang_c), jnp.cos(ang_c)], axis=-1)
    sin_t = jnp.concatenate([-jnp.sin(ang_r), jnp.sin(ang_r), -jnp.sin(ang_c), jnp.sin(ang_c)], axis=-1)
    return cos_t, sin_t


def _sgu_kernel(ug_ref, vg_ref, g_ref, ws_ref, bias_ref, o_ref):
    for c in range(ROW_TILE // CHUNK):
        rows = slice(c * CHUNK, (c + 1) * CHUNK)
        for g in range(SGU_GROUPS):
            cols = slice(g * LANES, (g + 1) * LANES)
            u = _gelu(ug_ref[rows, cols])
            vn = _ln(_gelu(vg_ref[rows, cols])) * g_ref[0:1, cols]
            mixed = jnp.dot(ws_ref[g], vn.astype(BF16), preferred_element_type=F32) + bias_ref[:, cols]
            o_ref[rows, cols] = u * mixed


def _spatial_gating(ug, vg, sgu_g, sgu_w, sgu_bias):
    t = ug.shape[0]
    row = pl.BlockSpec((ROW_TILE, SGU_W), lambda i: (i, 0))
    return pl.pallas_call(
        _sgu_kernel,
        grid=(t // ROW_TILE,),
        in_specs=[row, row,
                  pl.BlockSpec((1, SGU_W), lambda i: (0, 0)),
                  pl.BlockSpec((SGU_GROUPS, CHUNK, CHUNK), lambda i: (0, 0, 0)),
                  pl.BlockSpec((CHUNK, SGU_W), lambda i: (0, 0))],
        out_specs=row,
        out_shape=jax.ShapeDtypeStruct((t, SGU_W), F32),
        compiler_params=_cparams("arbitrary"),
        name="spatial_gating",
    )(ug, vg, sgu_g, sgu_w, sgu_bias)


def _fnet_kernel(f_ref, cc_ref, cs_ref, o_ref, z_ref, *, seq):
    @pl.when(pl.program_id(1) == 0)
    def _():
        cc = cc_ref[...]
        for g in range(FNET_GROUPS):
            cols = slice(g * FNET_GD, (g + 1) * FNET_GD)
            zz = jnp.dot(f_ref[:, cols].astype(BF16), cc, preferred_element_type=F32)
            z_ref[0:seq, cols] = zz[:, :FNET_GD].astype(BF16)
            z_ref[seq:2 * seq, cols] = zz[:, FNET_GD:].astype(BF16)

    y = jnp.dot(cs_ref[...], z_ref[...], preferred_element_type=F32)
    o_ref[...] = y * ((seq * FNET_GD) ** -0.5)


def _dft_tables(n):
    k = jnp.arange(n, dtype=jnp.int32)
    ang = ((k[:, None] * k[None, :]) % n).astype(F32) * (2.0 * jnp.pi / n)
    return jnp.cos(ang), jnp.sin(ang)


def _fourier_mix(f, batch, seq):
    t = f.shape[0]
    tr = min(seq, 512)
    c_c, s_c = _dft_tables(FNET_GD)
    c_s, s_s = _dft_tables(seq)
    cc = jnp.concatenate([c_c, s_c], axis=1).astype(BF16)
    cs = jnp.concatenate([c_s, -s_s], axis=1).astype(BF16)
    nj = seq // tr
    return pl.pallas_call(
        functools.partial(_fnet_kernel, seq=seq),
        grid=(batch, nj),
        in_specs=[pl.BlockSpec((seq, FNET_W), lambda b, j: (b, 0)),
                  pl.BlockSpec((FNET_GD, 2 * FNET_GD), lambda b, j: (0, 0)),
                  pl.BlockSpec((tr, 2 * seq), lambda b, j: (j, 0))],
        out_specs=pl.BlockSpec((tr, FNET_W), lambda b, j: (b * nj + j, 0)),
        out_shape=jax.ShapeDtypeStruct((t, FNET_W), F32),
        scratch_shapes=[pltpu.VMEM((2 * seq, FNET_W), BF16)],
        compiler_params=_cparams("arbitrary", "arbitrary"),
        name="fourier_mix",
    )(f, cc, cs)


def _mix_kernel(x_ref, a_ref, s_ref, f_ref, wo_ref, mod_ref, g1_ref, b1_ref, wq_ref,
                x1_ref, h2_ref, hs_ref, qp_ref):
    mix = jnp.dot(a_ref[...].astype(BF16), wo_ref[0:ATTN_W, :], preferred_element_type=F32)
    mix += jnp.dot(s_ref[...].astype(BF16), wo_ref[ATTN_W:ATTN_W + SGU_W, :], preferred_element_type=F32)
    mix += jnp.dot(f_ref[...].astype(BF16), wo_ref[ATTN_W + SGU_W:, :], preferred_element_type=F32)
    x1 = _ln(DN_ALPHA * x_ref[...] + mod_ref[2:3, :] * mix) * g1_ref[...] + b1_ref[...]
    x1_ref[...] = x1
    h2 = _ln(x1) * (1.0 + mod_ref[4:5, :]) + mod_ref[3:4, :]
    h2_ref[...], hs_ref[...] = _pack_fp8_cols(h2.T)
    q = jnp.dot(h2.astype(BF16), wq_ref[...], preferred_element_type=F32)
    for h in range(PEER_HEADS):
        qp_ref[h] = q[:, h * PEER_QDIM:(h + 1) * PEER_QDIM].astype(BF16)


def _out_projection(x, attn, sg, fn, w_out, mod, mod_base, rows_per_mod, ln_g, ln_b, wq):
    t = x.shape[0]
    tiles_per_mod = rows_per_mod // ROW_TILE
    row = lambda w: pl.BlockSpec((ROW_TILE, w), lambda i: (i, 0))
    full = lambda a, b: pl.BlockSpec((a, b), lambda i: (0, 0))
    return pl.pallas_call(
        _mix_kernel,
        grid=(t // ROW_TILE,),
        in_specs=[row(D_MODEL), row(ATTN_W), row(SGU_W), row(FNET_W),
                  full(D_MODEL, D_MODEL),
                  pl.BlockSpec((None, N_MOD, D_MODEL), lambda i: (mod_base + i // tiles_per_mod, 0, 0)),
                  full(1, D_MODEL), full(1, D_MODEL),
                  full(D_MODEL, PEER_HEADS * PEER_QDIM)],
        out_specs=[row(D_MODEL), pl.BlockSpec((D_MODEL // 4, ROW_TILE), lambda i: (0, i)),
                   pl.BlockSpec((1, ROW_TILE), lambda i: (0, i)),
                   pl.BlockSpec((PEER_HEADS, ROW_TILE, PEER_QDIM), lambda i: (0, i, 0))],
        out_shape=[jax.ShapeDtypeStruct((t, D_MODEL), F32),
                   jax.ShapeDtypeStruct((D_MODEL // 4, t), jnp.uint32),
                   jax.ShapeDtypeStruct((1, t), F32),
                   jax.ShapeDtypeStruct((PEER_HEADS, t, PEER_QDIM), BF16)],
        compiler_params=_cparams("arbitrary"),
        name="out_projection",
    )(x, attn, sg, fn, w_out, mod, ln_g, ln_b, wq)


def _pair_candidates(v1, v2):
    half = PEER_TOPK // 2
    pieces = [v1[0:1] + v2] + [v1[r1:r1 + 1] + v2[0:half] for r1 in range(1, PEER_TOPK)]
    pos_pieces = [lax.broadcasted_iota(jnp.int32, (PEER_TOPK, LANES), 0)]
    for r1 in range(1, PEER_TOPK):
        pos_pieces.append(lax.broadcasted_iota(jnp.int32, (half, LANES), 0) + r1 * PEER_TOPK)
    spans = [(0, PEER_TOPK)] + [(PEER_TOPK + (r1 - 1) * half, half) for r1 in range(1, PEER_TOPK)]
    return jnp.concatenate(pieces, axis=0), jnp.concatenate(pos_pieces, axis=0).astype(F32), spans


def _descending_maxima(slabs):
    w = slabs[0].shape[1]
    slot = lax.broadcasted_iota(jnp.int32, (PEER_TOPK, w), 0)

    def body(k, carry):
        out = []
        for s, (prev, vals) in zip(slabs, carry):
            m = jnp.max(jnp.where(s < prev, s, -jnp.inf), axis=0, keepdims=True)
            out.append((m, jnp.where(slot == k, m, vals)))
        return tuple(out)

    init = tuple((jnp.full((1, w), jnp.inf, F32), jnp.zeros((PEER_TOPK, w), F32)) for _ in slabs)
    return [vals for _, vals in lax.fori_loop(0, PEER_TOPK, body, init)]


def _count_rows(mask):
    return jnp.sum(jnp.where(mask, 1.0, 0.0), axis=0, keepdims=True)


def _topk_tables_distinct(s1, s2):
    nc = len(s1)
    k = float(PEER_TOPK)
    vals = _descending_maxima(s1 + s2)
    v1, v2 = vals[:nc], vals[nc:]
    cands = [_pair_candidates(a, b) for a, b in zip(v1, v2)]
    best = _descending_maxima([cand for cand, _, _ in cands])
    tables, bad = [], jnp.zeros((1, LANES), F32)
    for c in range(nc):
        cand, _, spans = cands[c]
        taken = jnp.where(cand >= best[c][PEER_TOPK - 1:PEER_TOPK], 1.0, 0.0)
        cnt = [jnp.sum(taken[lo:lo + n], axis=0, keepdims=True) for lo, n in spans]
        z = jnp.sum(jnp.exp(best[c] - best[c][0:1]), axis=0, keepdims=True)
        rank2 = jnp.zeros((PEER_NKEYS, LANES), F32)
        n1 = jnp.zeros((PEER_NKEYS, LANES), F32)
        for r in range(PEER_TOPK):
            rank2 = rank2 + jnp.where(v2[c][r:r + 1] > s2[c], 1.0, 0.0)
            n1 = jnp.where(s1[c] == v1[c][r:r + 1], cnt[r], n1)
        tables.append((rank2, jnp.exp(s2[c] - v2[c][0:1]), n1, jnp.exp(s1[c] - v1[c][0:1]) / z))
        for s, v in ((s1[c], v1[c]), (s2[c], v2[c])):
            bad = bad + jnp.abs(_count_rows(s >= v[PEER_TOPK - 1:PEER_TOPK]) - k)
        bad = bad + jnp.abs(sum(cnt) - k)
    return tables, bad


def _top16_rows(scores):
    n, w = scores[0].shape
    row = lax.broadcasted_iota(jnp.int32, (n, w), 0).astype(F32)
    slot = lax.broadcasted_iota(jnp.int32, (PEER_TOPK, w), 0)

    def body(k, carry):
        kf = lax.convert_element_type(k, F32)
        out = []
        for cur, rank, vals in carry:
            m = jnp.max(cur, axis=0, keepdims=True)
            hit = row == jnp.min(jnp.where(cur == m, row, float(n)), axis=0, keepdims=True)
            out.append((jnp.where(hit, -jnp.inf, cur), jnp.where(hit, kf, rank),
                        jnp.where(slot == k, m, vals)))
        return tuple(out)

    init = tuple((s, jnp.full((n, w), float(PEER_TOPK), F32), jnp.zeros((PEER_TOPK, w), F32))
                 for s in scores)
    res = lax.fori_loop(0, PEER_TOPK, body, init)
    return [(rank, vals) for _, rank, vals in res]


def _topk_tables_exact(s1, s2):
    nc = len(s1)
    tops = _top16_rows(s1 + s2)
    slot = lax.broadcasted_iota(jnp.int32, (PEER_TOPK, LANES), 0).astype(F32)
    cands = [_pair_candidates(tops[c][1], tops[nc + c][1]) for c in range(nc)]

    def body(k, carry):
        out = []
        for (cand, cnt, z), (cand0, pos, _) in zip(carry, cands):
            m = jnp.max(cand, axis=0, keepdims=True)
            p = jnp.min(jnp.where(cand == m, pos, 4096.0), axis=0, keepdims=True)
            out.append((jnp.where(pos == p, -jnp.inf, cand),
                        cnt + jnp.where(slot == jnp.floor(p * (1.0 / PEER_TOPK)), 1.0, 0.0),
                        z + jnp.exp(m - cand0[0:1])))
        return tuple(out)

    init = tuple((cand, jnp.zeros((PEER_TOPK, LANES), F32), jnp.zeros((1, LANES), F32))
                 for cand, _, _ in cands)
    res = lax.fori_loop(0, PEER_TOPK, body, init)
    tables = []
    for c in range(nc):
        (rank1, v1), (rank2, v2) = tops[c], tops[nc + c]
        _, cnt, z = res[c]
        n1 = jnp.zeros((PEER_NKEYS, LANES), F32)
        for r1 in range(PEER_TOPK):
            n1 = jnp.where(rank1 == float(r1), cnt[r1:r1 + 1], n1)
        tables.append((rank2, jnp.exp(s2[c] - v2[0:1]), n1, jnp.exp(s1[c] - v1[0:1]) / z))
    return tables


def _topk_kernel(q_ref, km_ref, r2_ref, e2_ref, n1_ref, e1_ref, *, tm):
    km = km_ref[...]
    toks = [slice(c * LANES, (c + 1) * LANES) for c in range(tm // LANES)]
    sts = [lax.dot_general(km, q_ref[tok, :], _NT, preferred_element_type=F32) for tok in toks]
    s1 = [st[0:PEER_NKEYS] for st in sts]
    s2 = [st[PEER_NKEYS:2 * PEER_NKEYS] for st in sts]

    def store(tables):
        for tok, (rank2, e2, n1, e1) in zip(toks, tables):
            r2_ref[:, tok] = _pack_bf16(rank2)
            e2_ref[:, tok] = _pack_bf16(e2)
            n1_ref[:, tok] = n1
            e1_ref[:, tok] = e1

    tables, bad = _topk_tables_distinct(s1, s2)
    store(tables)

    @pl.when(jnp.max(bad) > 0.0)
    def _():
        store(_topk_tables_exact(s1, s2))


def _peer_topk(qp, kmat):
    t = qp.shape[1]
    tm = min(TOPK_TOKEN_TILE, t)
    slab_spec = pl.BlockSpec((PEER_NKEYS // 2, tm), lambda i, h: (h, i))
    slab_shape = jax.ShapeDtypeStruct((PEER_HEADS * PEER_NKEYS // 2, t), jnp.uint32)
    row_spec = pl.BlockSpec((PEER_NKEYS, tm), lambda i, h: (h, i))
    row_shape = jax.ShapeDtypeStruct((PEER_HEADS * PEER_NKEYS, t), F32)
    return pl.pallas_call(
        functools.partial(_topk_kernel, tm=tm),
        grid=(t // tm, PEER_HEADS),
        in_specs=[pl.BlockSpec((None, tm, PEER_QDIM), lambda i, h: (h, i, 0)),
                  pl.BlockSpec((None, 2 * PEER_NKEYS, PEER_QDIM), lambda i, h: (h, 0, 0))],
        out_specs=[slab_spec, slab_spec, row_spec, row_spec],
        out_shape=[slab_shape, slab_shape, row_shape, row_shape],
        compiler_params=_cparams("arbitrary", "arbitrary"),
        name="peer_topk",
    )(qp, kmat)


def _peer_kernel(h_ref, hs_ref, u_ref, us_ref, vt_ref, r2_ref, e2_ref, n1_ref, e1_ref, o_ref, w_ref, *, tm):
    e = pl.program_id(1)
    words = PEER_NKEYS // 2

    @pl.when(e == 0)
    def _():
        o_ref[...] = jnp.zeros_like(o_ref)

    ht = _unpack_fp8(h_ref[...])
    for sb in range(PEER_EXPERT_TILE // PEER_EXPERT_SUBTILE):
        urows = slice(sb * PEER_EXPERT_SUBTILE // 4, (sb + 1) * PEER_EXPERT_SUBTILE // 4)
        at = jnp.dot(_unpack_fp8(u_ref[urows, :]), ht, preferred_element_type=F32)
        for bs in range(PEER_EXPERT_SUBTILE // PEER_NKEYS):
            bi = sb * (PEER_EXPERT_SUBTILE // PEER_NKEYS) + bs
            rows = slice(bi * words, (bi + 1) * words)
            half_us = 0.5 * us_ref[bi * PEER_NKEYS:(bi + 1) * PEER_NKEYS, :]
            for c in range(tm // LANES):
                tok = slice(c * LANES, (c + 1) * LANES)
                gate = jnp.zeros((PEER_NKEYS, LANES), BF16)
                for h in range(PEER_HEADS):
                    keys = slice(h * words, (h + 1) * words)
                    grp = pl.ds(pl.multiple_of(h * PEER_NKEYS + e * SUBLANES, SUBLANES), SUBLANES)
                    n1 = jnp.broadcast_to(n1_ref[grp, tok][bi:bi + 1, :], (PEER_NKEYS, LANES)).astype(BF16)
                    e1 = jnp.broadcast_to(e1_ref[grp, tok][bi:bi + 1, :], (PEER_NKEYS, LANES)).astype(BF16)
                    r2 = _unpack_bf16(r2_ref[keys, tok])
                    e2 = _unpack_bf16(e2_ref[keys, tok])
                    gate += jnp.where(r2 < n1, e2, 0.0) * e1
                y = at[bs * PEER_NKEYS:(bs + 1) * PEER_NKEYS, tok] * half_us * hs_ref[:, tok]
                act = (y * (1.0 + lax.erf(y * (2.0 ** 0.5)))).astype(BF16)
                w_ref[rows, tok] = pltpu.bitcast(act * gate, jnp.uint32)
    o_ref[...] += jnp.dot(_unpack_bf16(vt_ref[...]), _unpack_bf16(w_ref[...]), preferred_element_type=F32)


def _peer_apply(h2t, h2_scale, u, u_scale, vt, tables):
    t = h2t.shape[1]
    tm = min(PEER_TOKEN_TILE, t)
    eb = PEER_EXPERT_TILE
    slab = pl.BlockSpec((PEER_HEADS * PEER_NKEYS // 2, tm), lambda i, e: (0, i))
    rowt = pl.BlockSpec((PEER_HEADS * PEER_NKEYS, tm), lambda i, e: (0, i))
    return pl.pallas_call(
        functools.partial(_peer_kernel, tm=tm),
        grid=(t // tm, PEER_N // eb),
        in_specs=[pl.BlockSpec((D_MODEL // 4, tm), lambda i, e: (0, i)),
                  pl.BlockSpec((1, tm), lambda i, e: (0, i)),
                  pl.BlockSpec((eb // 4, D_MODEL), lambda i, e: (e, 0)),
                  pl.BlockSpec((eb, LANES), lambda i, e: (e, 0)),
                  pl.BlockSpec((D_MODEL // 2, eb), lambda i, e: (0, e)),
                  slab, slab, rowt, rowt],
        out_specs=pl.BlockSpec((D_MODEL, tm), lambda i, e: (0, i)),
        out_shape=jax.ShapeDtypeStruct((D_MODEL, t), F32),
        scratch_shapes=[pltpu.VMEM((eb // 2, tm), jnp.uint32)],
        compiler_params=_cparams("arbitrary", "arbitrary"),
        name="peer_apply",
    )(h2t, h2_scale, u, u_scale, vt, *tables)


def _final_kernel(x_ref, pt_ref, mod_ref, g_ref, b_ref, o_ref):
    y = DN_ALPHA * x_ref[...] + mod_ref[5:6, :] * pt_ref[...].T
    o_ref[...] = _ln(y) * g_ref[...] + b_ref[...]


def _finalize(x1, peer_t, mod, mod_base, rows_per_mod, ln_g, ln_b):
    t = x1.shape[0]
    tiles_per_mod = rows_per_mod // ROW_TILE
    row = pl.BlockSpec((ROW_TILE, D_MODEL), lambda i: (i, 0))
    vec = pl.BlockSpec((1, D_MODEL), lambda i: (0, 0))
    return pl.pallas_call(
        _final_kernel,
        grid=(t // ROW_TILE,),
        in_specs=[row,
                  pl.BlockSpec((D_MODEL, ROW_TILE), lambda i: (0, i)),
                  pl.BlockSpec((None, N_MOD, D_MODEL), lambda i: (mod_base + i // tiles_per_mod, 0, 0)),
                  vec, vec],
        out_specs=row,
        out_shape=jax.ShapeDtypeStruct((t, D_MODEL), F32),
        compiler_params=_cparams("arbitrary"),
        name="finalize",
    )(x1, peer_t, mod, ln_g, ln_b)


def _prep_layer(l, w_in, attn_sink, sgu_g, sgu_w, sgu_b, w_out, ln1_g, ln1_b,
                peer_wq, peer_keys, peer_u, peer_v, ln2_g, ln2_b):
    keys = peer_keys[l]
    zeros = jnp.zeros_like(keys[:, 0])
    kmat = jnp.concatenate([jnp.concatenate([keys[:, 0], zeros], axis=-1),
                            jnp.concatenate([zeros, keys[:, 1]], axis=-1)], axis=1)
    return dict(
        w_in=w_in[l].astype(BF16),
        sink_b=jnp.broadcast_to(
            jnp.pad(attn_sink[l].reshape(N_KV, GROUP), ((0, 0), (0, SUBLANES - GROUP)))[:, :, None],
            (N_KV, SUBLANES, LANES)),
        sgu_g=sgu_g[l].reshape(1, SGU_W),
        sgu_w=sgu_w[l].astype(BF16),
        sgu_bias=jnp.repeat(sgu_b[l].T, LANES, axis=1),
        w_out=w_out[l].astype(BF16),
        ln1_g=ln1_g[l].reshape(1, D_MODEL), ln1_b=ln1_b[l].reshape(1, D_MODEL),
        wq=peer_wq[l].astype(BF16),
        kmat=kmat.astype(BF16),
        u=_pack_table_fp8(peer_u, l),
        vt=_pack_table_transposed(peer_v, l),
        ln2_g=ln2_g[l].reshape(1, D_MODEL), ln2_b=ln2_b[l].reshape(1, D_MODEL),
    )


def _layer(x, batch, seq, mod, mod_base, rows_per_mod, lw, ctx):
    q, k, v, ug, vg, f = _in_projection(x, mod, mod_base, rows_per_mod, lw["w_in"])
    if ctx is None:
        attn = _context_attention(q, k, v, lw["sink_b"], batch, seq)
    else:
        cache_k, cache_v, layer, cos_t, sin_t = ctx
        attn = _latent_attention(q, k, v, cache_k, cache_v, layer, cos_t, sin_t, lw["sink_b"], batch, seq)
    sg = _spatial_gating(ug, vg, lw["sgu_g"], lw["sgu_w"], lw["sgu_bias"])
    fn = _fourier_mix(f, batch, seq)
    x1, h2t, h2_scale, qp = _out_projection(x, attn, sg, fn, lw["w_out"], mod, mod_base, rows_per_mod,
                                            lw["ln1_g"], lw["ln1_b"], lw["wq"])
    tables = _peer_topk(qp, lw["kmat"])
    peer_t = _peer_apply(h2t, h2_scale, *lw["u"], lw["vt"], tables)
    x2 = _finalize(x1, peer_t, mod, mod_base, rows_per_mod, lw["ln2_g"], lw["ln2_b"])
    return x2, k, v


def kernel(x_prompt, x_sample, cache_k, cache_v, c, c_ctx, w_mod, b_mod, w_in, attn_sink, sgu_g, sgu_w, sgu_b, w_out, ln1_g, ln1_b, peer_wq, peer_keys, peer_u, peer_v, ln2_g, ln2_b):
    batch, seq, _ = x_prompt.shape
    dec_batch, dec_seq, _ = x_sample.shape
    past = cache_k.shape[2]
    assert 1 + dec_batch <= MOD_ROWS

    cvec = jnp.concatenate([c_ctx[None, :], c, jnp.zeros((MOD_ROWS - 1 - dec_batch, D_MODEL), F32)], axis=0)
    mod = _modulation(cvec, w_mod, b_mod)
    cos_t, sin_t = _rope_tables(dec_seq)
    ck = cache_k.reshape(dec_batch, DEPTH, past, KV_W)
    cv = cache_v.reshape(dec_batch, DEPTH, past, KV_W)

    xp = x_prompt.reshape(batch * seq, D_MODEL)
    xs = x_sample.reshape(dec_batch * dec_seq, D_MODEL)
    new_k, new_v = [], []
    for l in range(DEPTH):
        lw = _prep_layer(l, w_in, attn_sink, sgu_g, sgu_w, sgu_b, w_out, ln1_g, ln1_b,
                         peer_wq, peer_keys, peer_u, peer_v, ln2_g, ln2_b)
        xp, kl, vl = _layer(xp, batch, seq, mod[l], 0, batch * seq, lw, None)
        new_k.append(kl.reshape(batch, seq, N_KV, HEAD_DIM))
        new_v.append(vl.reshape(batch, seq, N_KV, HEAD_DIM))
        xs, _, _ = _layer(xs, dec_batch, dec_seq, mod[l], 1, dec_seq, lw, (ck, cv, l, cos_t, sin_t))
    return (xp.reshape(batch, seq, D_MODEL), xs.reshape(dec_batch, dec_seq, D_MODEL),
            jnp.stack(new_k, axis=1), jnp.stack(new_v, axis=1))
```

```python
import functools

import jax
import jax.numpy as jnp
from jax import lax
from jax.experimental import pallas as pl
from jax.experimental.pallas import tpu as pltpu

D_MODEL = 2048
DEPTH = 2
GRID_W = 64
HEAD_DIM = 128
N_HEADS = 8
N_KV = 2
GROUP = N_HEADS // N_KV
ATTN_W = N_HEADS * HEAD_DIM
KV_W = N_KV * HEAD_DIM
SGU_GROUPS = 4
SGU_W = 512
FNET_GROUPS = 4
FNET_W = 512
FNET_GD = 128
PROJ_W = ATTN_W + 2 * KV_W + 2 * SGU_W + FNET_W
BLOCK = 128
WINDOW = 128
CHUNK = 128
ROPE_BASE = 10000.0
N_MOD = 6
PEER_HEADS = 8
PEER_NKEYS = 128
PEER_N = PEER_NKEYS * PEER_NKEYS
PEER_TOPK = 16
PEER_QDIM = 128
PEER_HALF = PEER_QDIM // 2
DN_ALPHA = (2 * DEPTH) ** 0.25
LN_EPS = 1e-5
NEG_INF = -1e30

F32 = jnp.float32
BF16 = jnp.bfloat16
FP8 = jnp.float8_e4m3fn
FP8_TOP = 256.0
LANES = 128
SUBLANES = 8
V7X_VMEM_BYTES = 64 * 2 ** 20
VMEM_LIMIT = V7X_VMEM_BYTES - 8 * 2 ** 20
ROW_TILE = 256
TOPK_TOKEN_TILE = 512
PEER_TOKEN_TILE = 512
PEER_EXPERT_SUBTILE = 128
PEER_OUT_ROWS = 256
PEER_EXPERT_TILE = SUBLANES * PEER_NKEYS
MOD_COL_TILE = 1024
MOD_ROWS = 16

_NT = (((1,), (1,)), ((), ()))


def _cparams(*sem):
    return pltpu.CompilerParams(dimension_semantics=sem, vmem_limit_bytes=VMEM_LIMIT)


def _row_reduce(op, lane_op, *blocks):
    acc = None
    for x in blocks:
        for i in range(x.shape[1] // LANES):
            chunk = x[:, i * LANES:(i + 1) * LANES]
            acc = chunk if acc is None else op(acc, chunk)
    return lane_op(acc, axis=-1, keepdims=True)


def _ln(x):
    inv_n = 1.0 / x.shape[-1]
    xc = x - _row_reduce(jnp.add, jnp.sum, x) * inv_n
    var = _row_reduce(jnp.add, jnp.sum, xc * xc) * inv_n
    return xc * lax.rsqrt(var + LN_EPS)


def _gelu(x):
    return 0.5 * x * (1.0 + lax.erf(x * (0.5 ** 0.5)))


def _pack_bf16(x):
    return pltpu.bitcast(x.astype(BF16), jnp.uint32)


def _unpack_bf16(words):
    return pltpu.bitcast(words, BF16)


def _fp8_scale(amax):
    return jnp.where(amax > 0.0, amax, FP8_TOP) * (1.0 / FP8_TOP)


def _pack_fp8(x):
    return pltpu.bitcast(x.astype(FP8), jnp.uint32)


def _unpack_fp8(words):
    return pltpu.bitcast(words, FP8)


def _pack_fp8_cols(xt):
    scale = _fp8_scale(jnp.max(jnp.abs(xt), axis=0, keepdims=True))
    return _pack_fp8(xt / scale), scale, jnp.sqrt(jnp.sum(xt * xt, axis=0, keepdims=True))


def _pack_rows_kernel(x_ref, o_ref, s_ref, n_ref, *, transpose):
    x = x_ref[...]
    scale = _fp8_scale(_row_reduce(jnp.maximum, jnp.max, jnp.abs(x)))
    xs = x / scale
    o_ref[...] = _pack_fp8(xs.T if transpose else xs)
    s_ref[...] = jnp.broadcast_to(scale, s_ref.shape)
    n_ref[...] = jnp.broadcast_to(jnp.sqrt(_row_reduce(jnp.add, jnp.sum, x * x)), n_ref.shape)


def _pack_table_fp8(table, layer, transpose):
    _, n, d = table.shape
    rows = PEER_EXPERT_TILE
    if transpose:
        out_spec = pl.BlockSpec((d // 4, rows), lambda e: (0, e))
        out_shape = jax.ShapeDtypeStruct((d // 4, n), jnp.uint32)
    else:
        out_spec = pl.BlockSpec((rows // 4, d), lambda e: (e, 0))
        out_shape = jax.ShapeDtypeStruct((n // 4, d), jnp.uint32)
    per_row = pl.BlockSpec((rows, LANES), lambda e: (e, 0))
    per_row_shape = jax.ShapeDtypeStruct((n, LANES), F32)
    return pl.pallas_call(
        functools.partial(_pack_rows_kernel, transpose=transpose),
        grid=(n // rows,),
        in_specs=[pl.BlockSpec((None, rows, d), lambda e: (layer, e, 0))],
        out_specs=[out_spec, per_row, per_row],
        out_shape=[out_shape, per_row_shape, per_row_shape],
        compiler_params=_cparams("arbitrary"),
        name="pack_table_fp8_t" if transpose else "pack_table_fp8",
    )(table)


def _mod_kernel(c_ref, w_ref, b_ref, o_ref):
    c = c_ref[...]
    a = (c / (1.0 + jnp.exp(-c))).astype(BF16)
    o_ref[...] = jnp.dot(a, w_ref[...].astype(BF16), preferred_element_type=F32) + b_ref[...]


def _modulation(cvec, w_mod, b_mod):
    n = N_MOD * D_MODEL
    out = pl.pallas_call(
        _mod_kernel,
        grid=(DEPTH, n // MOD_COL_TILE),
        in_specs=[
            pl.BlockSpec((MOD_ROWS, D_MODEL), lambda l, j: (0, 0)),
            pl.BlockSpec((None, D_MODEL, MOD_COL_TILE), lambda l, j: (l, 0, j)),
            pl.BlockSpec((None, 1, MOD_COL_TILE), lambda l, j: (l, 0, j)),
        ],
        out_specs=pl.BlockSpec((None, MOD_ROWS, MOD_COL_TILE), lambda l, j: (l, 0, j)),
        out_shape=jax.ShapeDtypeStruct((DEPTH, MOD_ROWS, n), F32),
        compiler_params=_cparams("arbitrary", "arbitrary"),
        name="modulation",
    )(cvec, w_mod, b_mod.reshape(DEPTH, 1, n))
    return out.reshape(DEPTH, MOD_ROWS, N_MOD, D_MODEL)


_PROJ_SLICES = ((0, ATTN_W), (ATTN_W, KV_W), (ATTN_W + KV_W, KV_W),
                (ATTN_W + 2 * KV_W, SGU_W), (ATTN_W + 2 * KV_W + SGU_W, SGU_W),
                (ATTN_W + 2 * KV_W + 2 * SGU_W, FNET_W))


def _proj_kernel(x_ref, mod_ref, w_ref, *out_refs):
    h = _ln(x_ref[...]) * (1.0 + mod_ref[1:2, :]) + mod_ref[0:1, :]
    h = h.astype(BF16)
    for (start, width), o_ref in zip(_PROJ_SLICES, out_refs):
        o_ref[...] = jnp.dot(h, w_ref[:, start:start + width], preferred_element_type=F32)


def _in_projection(x, mod, mod_base, rows_per_mod, w_in):
    t = x.shape[0]
    tiles_per_mod = rows_per_mod // ROW_TILE
    row = lambda w: pl.BlockSpec((ROW_TILE, w), lambda i: (i, 0))
    return pl.pallas_call(
        _proj_kernel,
        grid=(t // ROW_TILE,),
        in_specs=[
            row(D_MODEL),
            pl.BlockSpec((None, N_MOD, D_MODEL), lambda i: (mod_base + i // tiles_per_mod, 0, 0)),
            pl.BlockSpec((D_MODEL, PROJ_W), lambda i: (0, 0)),
        ],
        out_specs=[row(w) for _, w in _PROJ_SLICES],
        out_shape=[jax.ShapeDtypeStruct((t, w), F32) for _, w in _PROJ_SLICES],
        compiler_params=_cparams("arbitrary"),
        name="in_projection",
    )(x, mod, w_in)


def _head_cols(g):
    return slice(g * HEAD_DIM, (g + 1) * HEAD_DIM)


def _stacked_sinks(sink_ref, rows):
    return jnp.concatenate([jnp.broadcast_to(sink_ref[g:g + 1, 0:1], (rows, 1)) for g in range(GROUP)],
                           axis=0)


def _ctx_attn_kernel(q_ref, k_ref, v_ref, sink_ref, o_ref):
    seq = q_ref.shape[0]
    k = k_ref[...].astype(BF16)
    v = v_ref[...].astype(BF16)
    q = jnp.concatenate([q_ref[:, _head_cols(g)] for g in range(GROUP)], axis=0).astype(BF16)
    s = lax.dot_general(q, k, _NT, preferred_element_type=F32) * (HEAD_DIM ** -0.5)
    sk = _stacked_sinks(sink_ref, seq)
    m = jnp.maximum(_row_reduce(jnp.maximum, jnp.max, s), sk)
    p = jnp.exp(s - m)
    den = _row_reduce(jnp.add, jnp.sum, p) + jnp.exp(sk - m)
    o = jnp.dot(p.astype(BF16), v, preferred_element_type=F32) / den
    for g in range(GROUP):
        o_ref[:, _head_cols(g)] = o[g * seq:(g + 1) * seq]


def _context_attention(q, k, v, sink_b, batch, seq):
    t = q.shape[0]
    qspec = pl.BlockSpec((seq, GROUP * HEAD_DIM), lambda b, h: (b, h))
    kvspec = pl.BlockSpec((seq, HEAD_DIM), lambda b, h: (b, h))
    return pl.pallas_call(
        _ctx_attn_kernel,
        grid=(batch, N_KV),
        in_specs=[qspec, kvspec, kvspec, pl.BlockSpec((None, SUBLANES, LANES), lambda b, h: (h, 0, 0))],
        out_specs=qspec,
        out_shape=jax.ShapeDtypeStruct((t, ATTN_W), F32),
        compiler_params=_cparams("arbitrary", "arbitrary"),
        name="context_attention",
    )(q, k, v, sink_b)


def _lat_attn_kernel(q_ref, k_ref, v_ref, ck_ref, cv_ref, cos_ref, sin_ref, sink_ref, o_ref,
                     kp_ref, vp_ref, *, seq):
    scale = HEAD_DIM ** -0.5
    lane = lax.broadcasted_iota(jnp.int32, (1, HEAD_DIM), 1)
    first = (lane % 64) < 32

    def rope(x, cos, sin):
        sw = jnp.where(first, pltpu.roll(x, 96, 1), pltpu.roll(x, 32, 1))
        return x * cos + sw * sin

    pad = jnp.zeros((BLOCK, HEAD_DIM), BF16)
    kp_ref[0:BLOCK, :] = pad
    kp_ref[seq + BLOCK:seq + 2 * BLOCK, :] = pad
    vp_ref[0:BLOCK, :] = pad
    vp_ref[seq + BLOCK:seq + 2 * BLOCK, :] = pad
    kp_ref[BLOCK:seq + BLOCK, :] = rope(k_ref[...], cos_ref[...], sin_ref[...]).astype(BF16)
    vp_ref[BLOCK:seq + BLOCK, :] = v_ref[...].astype(BF16)
    ck = ck_ref[...].astype(BF16)
    cv = cv_ref[...].astype(BF16)
    sk = _stacked_sinks(sink_ref, BLOCK)

    def body(qb, carry):
        r0 = pl.multiple_of(qb * BLOCK, BLOCK)
        kb = kp_ref[pl.ds(r0, 3 * BLOCK), :]
        vb = vp_ref[pl.ds(r0, 3 * BLOCK), :]
        cosq = cos_ref[pl.ds(r0, BLOCK), :]
        sinq = sin_ref[pl.ds(r0, BLOCK), :]
        r = lax.broadcasted_iota(jnp.int32, (GROUP * BLOCK, 3 * BLOCK), 0) & (BLOCK - 1)
        j = lax.broadcasted_iota(jnp.int32, (GROUP * BLOCK, 3 * BLOCK), 1)
        kpos = j + (r0 - BLOCK)
        valid = (j >= r) & (j <= r + 2 * WINDOW) & (kpos >= 0) & (kpos < seq)
        q = jnp.concatenate([rope(q_ref[pl.ds(r0, BLOCK), _head_cols(g)], cosq, sinq)
                             for g in range(GROUP)], axis=0).astype(BF16)
        sl = lax.dot_general(q, kb, _NT, preferred_element_type=F32) * scale
        sl = jnp.where(valid, sl, NEG_INF)
        sc = lax.dot_general(q, ck, _NT, preferred_element_type=F32) * scale
        m = jnp.maximum(_row_reduce(jnp.maximum, jnp.max, sl, sc), sk)
        p_l = jnp.exp(sl - m)
        p_c = jnp.exp(sc - m)
        den = _row_reduce(jnp.add, jnp.sum, p_l, p_c) + jnp.exp(sk - m)
        o = (jnp.dot(p_l.astype(BF16), vb, preferred_element_type=F32)
             + jnp.dot(p_c.astype(BF16), cv, preferred_element_type=F32)) / den
        for g in range(GROUP):
            o_ref[pl.ds(r0, BLOCK), _head_cols(g)] = o[g * BLOCK:(g + 1) * BLOCK]
        return carry

    lax.fori_loop(0, seq // BLOCK, body, 0, unroll=4)


def _latent_attention(q, k, v, cache_k, cache_v, layer, cos_t, sin_t, sink_b, batch, seq):
    t = q.shape[0]
    past = cache_k.shape[2]
    qspec = pl.BlockSpec((seq, GROUP * HEAD_DIM), lambda b, h: (b, h))
    kvspec = pl.BlockSpec((seq, HEAD_DIM), lambda b, h: (b, h))
    cspec = pl.BlockSpec((None, None, past, HEAD_DIM), lambda b, h: (b, layer, 0, h))
    tspec = pl.BlockSpec((seq, HEAD_DIM), lambda b, h: (0, 0))
    return pl.pallas_call(
        functools.partial(_lat_attn_kernel, seq=seq),
        grid=(batch, N_KV),
        in_specs=[qspec, kvspec, kvspec, cspec, cspec, tspec, tspec,
                  pl.BlockSpec((None, SUBLANES, LANES), lambda b, h: (h, 0, 0))],
        out_specs=qspec,
        out_shape=jax.ShapeDtypeStruct((t, ATTN_W), F32),
        scratch_shapes=[pltpu.VMEM((seq + 2 * BLOCK, HEAD_DIM), BF16),
                        pltpu.VMEM((seq + 2 * BLOCK, HEAD_DIM), BF16)],
        compiler_params=_cparams("arbitrary", "arbitrary"),
        name="latent_attention",
    )(q, k, v, cache_k, cache_v, cos_t, sin_t, sink_b)


def _rope_tables(seq):
    pos = jnp.arange(seq)
    r = (pos // GRID_W).astype(F32)
    col = (pos % GRID_W).astype(F32)
    n = HEAD_DIM // 4
    inv = ROPE_BASE ** (-jnp.arange(n, dtype=F32) / n)
    ang_r = r[:, None] * inv
    ang_c = col[:, None] * inv
    cos_t = jnp.concatenate([jnp.cos(ang_r), jnp.cos(ang_r), jnp.cos(ang_c), jnp.cos(ang_c)], axis=-1)
    sin_t = jnp.concatenate([-jnp.sin(ang_r), jnp.sin(ang_r), -jnp.sin(ang_c), jnp.sin(ang_c)], axis=-1)
    return cos_t, sin_t


def _sgu_kernel(ug_ref, vg_ref, g_ref, ws_ref, bias_ref, o_ref):
    for c in range(ROW_TILE // CHUNK):
        rows = slice(c * CHUNK, (c + 1) * CHUNK)
        for g in range(SGU_GROUPS):
            cols = slice(g * LANES, (g + 1) * LANES)
            u = _gelu(ug_ref[rows, cols])
            vn = _ln(_gelu(vg_ref[rows, cols])) * g_ref[0:1, cols]
            mixed = jnp.dot(ws_ref[g], vn.astype(BF16), preferred_element_type=F32) + bias_ref[:, cols]
            o_ref[rows, cols] = u * mixed


def _spatial_gating(ug, vg, sgu_g, sgu_w, sgu_bias):
    t = ug.shape[0]
    row = pl.BlockSpec((ROW_TILE, SGU_W), lambda i: (i, 0))
    return pl.pallas_call(
        _sgu_kernel,
        grid=(t // ROW_TILE,),
        in_specs=[row, row,
                  pl.BlockSpec((1, SGU_W), lambda i: (0, 0)),
                  pl.BlockSpec((SGU_GROUPS, CHUNK, CHUNK), lambda i: (0, 0, 0)),
                  pl.BlockSpec((CHUNK, SGU_W), lambda i: (0, 0))],
        out_specs=row,
        out_shape=jax.ShapeDtypeStruct((t, SGU_W), F32),
        compiler_params=_cparams("arbitrary"),
        name="spatial_gating",
    )(ug, vg, sgu_g, sgu_w, sgu_bias)


def _fnet_kernel(f_ref, cc_ref, cs_ref, o_ref, z_ref, *, seq):
    @pl.when(pl.program_id(1) == 0)
    def _():
        cc = cc_ref[...]
        for g in range(FNET_GROUPS):
            cols = slice(g * FNET_GD, (g + 1) * FNET_GD)
            zz = jnp.dot(f_ref[:, cols].astype(BF16), cc, preferred_element_type=F32)
            z_ref[0:seq, cols] = zz[:, :FNET_GD].astype(BF16)
            z_ref[seq:2 * seq, cols] = zz[:, FNET_GD:].astype(BF16)

    y = jnp.dot(cs_ref[...], z_ref[...], preferred_element_type=F32)
    o_ref[...] = y * ((seq * FNET_GD) ** -0.5)


def _dft_tables(n):
    k = jnp.arange(n, dtype=jnp.int32)
    ang = ((k[:, None] * k[None, :]) % n).astype(F32) * (2.0 * jnp.pi / n)
    return jnp.cos(ang), jnp.sin(ang)


def _fourier_mix(f, batch, seq):
    t = f.shape[0]
    tr = min(seq, 512)
    c_c, s_c = _dft_tables(FNET_GD)
    c_s, s_s = _dft_tables(seq)
    cc = jnp.concatenate([c_c, s_c], axis=1).astype(BF16)
    cs = jnp.concatenate([c_s, -s_s], axis=1).astype(BF16)
    nj = seq // tr
    return pl.pallas_call(
        functools.partial(_fnet_kernel, seq=seq),
        grid=(batch, nj),
        in_specs=[pl.BlockSpec((seq, FNET_W), lambda b, j: (b, 0)),
                  pl.BlockSpec((FNET_GD, 2 * FNET_GD), lambda b, j: (0, 0)),
                  pl.BlockSpec((tr, 2 * seq), lambda b, j: (j, 0))],
        out_specs=pl.BlockSpec((tr, FNET_W), lambda b, j: (b * nj + j, 0)),
        out_shape=jax.ShapeDtypeStruct((t, FNET_W), F32),
        scratch_shapes=[pltpu.VMEM((2 * seq, FNET_W), BF16)],
        compiler_params=_cparams("arbitrary", "arbitrary"),
        name="fourier_mix",
    )(f, cc, cs)


def _mix_kernel(x_ref, a_ref, s_ref, f_ref, wo_ref, mod_ref, g1_ref, b1_ref, wq_ref,
                x1_ref, h2_ref, hs_ref, hn_ref, qp_ref):
    mix = jnp.dot(a_ref[...].astype(BF16), wo_ref[0:ATTN_W, :], preferred_element_type=F32)
    mix += jnp.dot(s_ref[...].astype(BF16), wo_ref[ATTN_W:ATTN_W + SGU_W, :], preferred_element_type=F32)
    mix += jnp.dot(f_ref[...].astype(BF16), wo_ref[ATTN_W + SGU_W:, :], preferred_element_type=F32)
    x1 = _ln(DN_ALPHA * x_ref[...] + mod_ref[2:3, :] * mix) * g1_ref[...] + b1_ref[...]
    x1_ref[...] = x1
    h2 = _ln(x1) * (1.0 + mod_ref[4:5, :]) + mod_ref[3:4, :]
    h2_ref[...], hs_ref[...], hn_ref[...] = _pack_fp8_cols(h2.T)
    q = jnp.dot(h2.astype(BF16), wq_ref[...], preferred_element_type=F32)
    for h in range(PEER_HEADS):
        qp_ref[h] = q[:, h * PEER_QDIM:(h + 1) * PEER_QDIM].astype(BF16)


def _out_projection(x, attn, sg, fn, w_out, mod, mod_base, rows_per_mod, ln_g, ln_b, wq):
    t = x.shape[0]
    tiles_per_mod = rows_per_mod // ROW_TILE
    row = lambda w: pl.BlockSpec((ROW_TILE, w), lambda i: (i, 0))
    full = lambda a, b: pl.BlockSpec((a, b), lambda i: (0, 0))
    return pl.pallas_call(
        _mix_kernel,
        grid=(t // ROW_TILE,),
        in_specs=[row(D_MODEL), row(ATTN_W), row(SGU_W), row(FNET_W),
                  full(D_MODEL, D_MODEL),
                  pl.BlockSpec((None, N_MOD, D_MODEL), lambda i: (mod_base + i // tiles_per_mod, 0, 0)),
                  full(1, D_MODEL), full(1, D_MODEL),
                  full(D_MODEL, PEER_HEADS * PEER_QDIM)],
        out_specs=[row(D_MODEL), pl.BlockSpec((D_MODEL // 4, ROW_TILE), lambda i: (0, i)),
                   pl.BlockSpec((1, ROW_TILE), lambda i: (0, i)),
                   pl.BlockSpec((1, ROW_TILE), lambda i: (0, i)),
                   pl.BlockSpec((PEER_HEADS, ROW_TILE, PEER_QDIM), lambda i: (0, i, 0))],
        out_shape=[jax.ShapeDtypeStruct((t, D_MODEL), F32),
                   jax.ShapeDtypeStruct((D_MODEL // 4, t), jnp.uint32),
                   jax.ShapeDtypeStruct((1, t), F32),
                   jax.ShapeDtypeStruct((1, t), F32),
                   jax.ShapeDtypeStruct((PEER_HEADS, t, PEER_QDIM), BF16)],
        compiler_params=_cparams("arbitrary"),
        name="out_projection",
    )(x, attn, sg, fn, w_out, mod, ln_g, ln_b, wq)


def _pair_candidates(v1, v2):
    half = PEER_TOPK // 2
    pieces = [v1[0:1] + v2] + [v1[r1:r1 + 1] + v2[0:half] for r1 in range(1, PEER_TOPK)]
    pos_pieces = [lax.broadcasted_iota(jnp.int32, (PEER_TOPK, LANES), 0)]
    for r1 in range(1, PEER_TOPK):
        pos_pieces.append(lax.broadcasted_iota(jnp.int32, (half, LANES), 0) + r1 * PEER_TOPK)
    spans = [(0, PEER_TOPK)] + [(PEER_TOPK + (r1 - 1) * half, half) for r1 in range(1, PEER_TOPK)]
    return jnp.concatenate(pieces, axis=0), jnp.concatenate(pos_pieces, axis=0).astype(F32), spans


def _descending_maxima(slabs):
    w = slabs[0].shape[1]
    slot = lax.broadcasted_iota(jnp.int32, (PEER_TOPK, w), 0)

    def body(k, carry):
        out = []
        for s, (prev, vals) in zip(slabs, carry):
            m = jnp.max(jnp.where(s < prev, s, -jnp.inf), axis=0, keepdims=True)
            out.append((m, jnp.where(slot == k, m, vals)))
        return tuple(out)

    init = tuple((jnp.full((1, w), jnp.inf, F32), jnp.zeros((PEER_TOPK, w), F32)) for _ in slabs)
    return [vals for _, vals in lax.fori_loop(0, PEER_TOPK, body, init)]


def _count_rows(mask):
    return jnp.sum(jnp.where(mask, 1.0, 0.0), axis=0, keepdims=True)


def _topk_tables_distinct(s1, s2):
    nc = len(s1)
    k = float(PEER_TOPK)
    vals = _descending_maxima(s1 + s2)
    v1, v2 = vals[:nc], vals[nc:]
    cands = [_pair_candidates(a, b) for a, b in zip(v1, v2)]
    best = _descending_maxima([cand for cand, _, _ in cands])
    tables, bad = [], jnp.zeros((1, LANES), F32)
    for c in range(nc):
        cand, _, spans = cands[c]
        taken = jnp.where(cand >= best[c][PEER_TOPK - 1:PEER_TOPK], 1.0, 0.0)
        cnt = [jnp.sum(taken[lo:lo + n], axis=0, keepdims=True) for lo, n in spans]
        z = jnp.sum(jnp.exp(best[c] - best[c][0:1]), axis=0, keepdims=True)
        rank2 = jnp.zeros((PEER_NKEYS, LANES), F32)
        n1 = jnp.zeros((PEER_NKEYS, LANES), F32)
        for r in range(PEER_TOPK):
            rank2 = rank2 + jnp.where(v2[c][r:r + 1] > s2[c], 1.0, 0.0)
            n1 = jnp.where(s1[c] == v1[c][r:r + 1], cnt[r], n1)
        tables.append((rank2, jnp.exp(s2[c] - v2[c][0:1]), n1, jnp.exp(s1[c] - v1[c][0:1]) / z))
        for s, v in ((s1[c], v1[c]), (s2[c], v2[c])):
            bad = bad + jnp.abs(_count_rows(s >= v[PEER_TOPK - 1:PEER_TOPK]) - k)
        bad = bad + jnp.abs(sum(cnt) - k)
    return tables, bad


def _top16_rows(scores):
    n, w = scores[0].shape
    row = lax.broadcasted_iota(jnp.int32, (n, w), 0).astype(F32)
    slot = lax.broadcasted_iota(jnp.int32, (PEER_TOPK, w), 0)

    def body(k, carry):
        kf = lax.convert_element_type(k, F32)
        out = []
        for cur, rank, vals in carry:
            m = jnp.max(cur, axis=0, keepdims=True)
            hit = row == jnp.min(jnp.where(cur == m, row, float(n)), axis=0, keepdims=True)
            out.append((jnp.where(hit, -jnp.inf, cur), jnp.where(hit, kf, rank),
                        jnp.where(slot == k, m, vals)))
        return tuple(out)

    init = tuple((s, jnp.full((n, w), float(PEER_TOPK), F32), jnp.zeros((PEER_TOPK, w), F32))
                 for s in scores)
    res = lax.fori_loop(0, PEER_TOPK, body, init)
    return [(rank, vals) for _, rank, vals in res]


def _topk_tables_exact(s1, s2):
    nc = len(s1)
    tops = _top16_rows(s1 + s2)
    slot = lax.broadcasted_iota(jnp.int32, (PEER_TOPK, LANES), 0).astype(F32)
    cands = [_pair_candidates(tops[c][1], tops[nc + c][1]) for c in range(nc)]

    def body(k, carry):
        out = []
        for (cand, cnt, z), (cand0, pos, _) in zip(carry, cands):
            m = jnp.max(cand, axis=0, keepdims=True)
            p = jnp.min(jnp.where(cand == m, pos, 4096.0), axis=0, keepdims=True)
            out.append((jnp.where(pos == p, -jnp.inf, cand),
                        cnt + jnp.where(slot == jnp.floor(p * (1.0 / PEER_TOPK)), 1.0, 0.0),
                        z + jnp.exp(m - cand0[0:1])))
        return tuple(out)

    init = tuple((cand, jnp.zeros((PEER_TOPK, LANES), F32), jnp.zeros((1, LANES), F32))
                 for cand, _, _ in cands)
    res = lax.fori_loop(0, PEER_TOPK, body, init)
    tables = []
    for c in range(nc):
        (rank1, v1), (rank2, v2) = tops[c], tops[nc + c]
        _, cnt, z = res[c]
        n1 = jnp.zeros((PEER_NKEYS, LANES), F32)
        for r1 in range(PEER_TOPK):
            n1 = jnp.where(rank1 == float(r1), cnt[r1:r1 + 1], n1)
        tables.append((rank2, jnp.exp(s2[c] - v2[0:1]), n1, jnp.exp(s1[c] - v1[0:1]) / z))
    return tables


def _topk_kernel(q_ref, km_ref, r2_ref, e2_ref, n1_ref, e1_ref, *, tm):
    km = km_ref[...]
    toks = [slice(c * LANES, (c + 1) * LANES) for c in range(tm // LANES)]
    sts = [lax.dot_general(km, q_ref[tok, :], _NT, preferred_element_type=F32) for tok in toks]
    s1 = [st[0:PEER_NKEYS] for st in sts]
    s2 = [st[PEER_NKEYS:2 * PEER_NKEYS] for st in sts]

    def store(tables):
        for tok, (rank2, e2, n1, e1) in zip(toks, tables):
            r2_ref[:, tok] = _pack_bf16(rank2)
            e2_ref[:, tok] = _pack_bf16(e2)
            n1_ref[:, tok] = n1
            e1_ref[:, tok] = e1

    tables, bad = _topk_tables_distinct(s1, s2)
    store(tables)

    @pl.when(jnp.max(bad) > 0.0)
    def _():
        store(_topk_tables_exact(s1, s2))


def _peer_topk(qp, kmat):
    t = qp.shape[1]
    tm = min(TOPK_TOKEN_TILE, t)
    slab_spec = pl.BlockSpec((PEER_NKEYS // 2, tm), lambda i, h: (h, i))
    slab_shape = jax.ShapeDtypeStruct((PEER_HEADS * PEER_NKEYS // 2, t), jnp.uint32)
    row_spec = pl.BlockSpec((PEER_NKEYS, tm), lambda i, h: (h, i))
    row_shape = jax.ShapeDtypeStruct((PEER_HEADS * PEER_NKEYS, t), F32)
    return pl.pallas_call(
        functools.partial(_topk_kernel, tm=tm),
        grid=(t // tm, PEER_HEADS),
        in_specs=[pl.BlockSpec((None, tm, PEER_QDIM), lambda i, h: (h, i, 0)),
                  pl.BlockSpec((None, 2 * PEER_NKEYS, PEER_QDIM), lambda i, h: (h, 0, 0))],
        out_specs=[slab_spec, slab_spec, row_spec, row_spec],
        out_shape=[slab_shape, slab_shape, row_shape, row_shape],
        compiler_params=_cparams("arbitrary", "arbitrary"),
        name="peer_topk",
    )(qp, kmat)


def _peer_kernel(h_ref, hs_ref, iw_ref, u_ref, us_ref, vt_ref, vs_ref, r2_ref, e2_ref, n1_ref, e1_ref,
                 o_ref, g_ref, w_ref, *, tm, n_steps):
    e = pl.program_id(1)

    @pl.when(e == 0)
    def _():
        o_ref[...] = jnp.zeros_like(o_ref)
        _peer_gated_activations(e, 0, h_ref, hs_ref, iw_ref, u_ref, us_ref, vs_ref, r2_ref, e2_ref,
                                n1_ref, e1_ref, g_ref, w_ref, tm)

    @pl.when((e > 0) & (e < n_steps))
    def _():
        chunks = _peer_accumulate_chunks((e - 1) % 2, vt_ref, w_ref, o_ref)
        _peer_gated_activations(e, e % 2, h_ref, hs_ref, iw_ref, u_ref, us_ref, vs_ref, r2_ref, e2_ref,
                                n1_ref, e1_ref, g_ref, w_ref, tm, between=chunks)

    @pl.when(e == n_steps)
    def _():
        _peer_accumulate((n_steps - 1) % 2, vt_ref, w_ref, o_ref)


def _peer_accumulate_chunks(slot, vt_ref, w_ref, o_ref):
    def chunk(r0):
        rows = slice(r0, r0 + PEER_OUT_ROWS)
        o_ref[rows, :] += jnp.dot(_unpack_fp8(vt_ref[r0 // 4:(r0 + PEER_OUT_ROWS) // 4, :]),
                                  _unpack_fp8(w_ref[slot]), preferred_element_type=F32)
    return [functools.partial(chunk, r0) for r0 in range(0, D_MODEL, PEER_OUT_ROWS)]


def _peer_accumulate(slot, vt_ref, w_ref, o_ref):
    for chunk in _peer_accumulate_chunks(slot, vt_ref, w_ref, o_ref):
        chunk()


def _peer_gated_activations(e, slot, h_ref, hs_ref, iw_ref, u_ref, us_ref, vs_ref, r2_ref, e2_ref,
                            n1_ref, e1_ref, g_ref, w_ref, tm, between=()):
    between = list(between)
    assert len(between) in (0, SUBLANES)
    words = PEER_NKEYS // 2
    quads = PEER_NKEYS // 4
    pack = 2 * SUBLANES
    groups = PEER_NKEYS // pack

    for bi in range(SUBLANES):
        experts = slice(bi * PEER_NKEYS, (bi + 1) * PEER_NKEYS)
        for c in range(tm // LANES):
            tok = slice(c * LANES, (c + 1) * LANES)
            gate = jnp.zeros((groups, pack, LANES), BF16)
            for h in range(PEER_HEADS):
                keys = slice(h * words, (h + 1) * words)
                grp = pl.ds(pl.multiple_of(h * PEER_NKEYS + e * SUBLANES, SUBLANES), SUBLANES)
                n1 = jnp.broadcast_to(n1_ref[grp, tok][bi:bi + 1, :], (pack, LANES)).astype(BF16)
                e1 = jnp.broadcast_to(e1_ref[grp, tok][bi:bi + 1, :], (pack, LANES)).astype(BF16)
                r2 = _unpack_bf16(r2_ref[keys, tok]).reshape(groups, pack, LANES)
                e2 = _unpack_bf16(e2_ref[keys, tok]).reshape(groups, pack, LANES)
                gate += jnp.where(r2 < n1[None], e2, 0.0) * e1[None]
            rescale = (vs_ref[experts, :] * iw_ref[:, tok]).astype(BF16)
            g_ref[bi * words:(bi + 1) * words, tok] = pltpu.bitcast(
                gate.reshape(PEER_NKEYS, LANES) * rescale, jnp.uint32)
        if between:
            between[bi]()

    ht = _unpack_fp8(h_ref[...])
    per_sub = PEER_EXPERT_SUBTILE // PEER_NKEYS
    for sb in range(PEER_EXPERT_TILE // PEER_EXPERT_SUBTILE):
        urows = slice(sb * PEER_EXPERT_SUBTILE // 4, (sb + 1) * PEER_EXPERT_SUBTILE // 4)
        at = jnp.dot(_unpack_fp8(u_ref[urows, :]), ht, preferred_element_type=F32)
        for bs in range(per_sub):
            bi = sb * per_sub + bs
            half_us = 0.5 * us_ref[bi * PEER_NKEYS:(bi + 1) * PEER_NKEYS, :]
            for c in range(tm // LANES):
                tok = slice(c * LANES, (c + 1) * LANES)
                y = at[bs * PEER_NKEYS:(bs + 1) * PEER_NKEYS, tok] * half_us * hs_ref[:, tok]
                act = (y * (1.0 + lax.erf(y * (2.0 ** 0.5)))).astype(BF16)
                gate = _unpack_bf16(g_ref[bi * words:(bi + 1) * words, tok])
                w_ref[slot, bi * quads:(bi + 1) * quads, tok] = pltpu.bitcast((act * gate).astype(FP8),
                                                                                jnp.uint32)


def _peer_weight_scale(h2_norm, u_norm, v_scale):
    bound = PEER_HEADS * jnp.max(u_norm[:, 0] * v_scale[:, 0]) * h2_norm
    return jnp.where(bound > 0.0, bound, FP8_TOP) * (1.0 / FP8_TOP)


def _peer_apply(h2t, h2_scale, w_scale, u, u_scale, vt, v_scale, tables):
    t = h2t.shape[1]
    tm = min(PEER_TOKEN_TILE, t)
    eb = PEER_EXPERT_TILE
    slab = pl.BlockSpec((PEER_HEADS * PEER_NKEYS // 2, tm), lambda i, e: (0, i))
    rowt = pl.BlockSpec((PEER_HEADS * PEER_NKEYS, tm), lambda i, e: (0, i))
    n_steps = PEER_N // eb
    per_token = pl.BlockSpec((1, tm), lambda i, e: (0, i))
    this_block = lambda e: jnp.minimum(e, n_steps - 1)
    prev_block = lambda e: jnp.maximum(e - 1, 0)
    per_expert = pl.BlockSpec((eb, LANES), lambda i, e: (this_block(e), 0))
    return pl.pallas_call(
        functools.partial(_peer_kernel, tm=tm, n_steps=n_steps),
        grid=(t // tm, n_steps + 1),
        in_specs=[pl.BlockSpec((D_MODEL // 4, tm), lambda i, e: (0, i)), per_token, per_token,
                  pl.BlockSpec((eb // 4, D_MODEL), lambda i, e: (this_block(e), 0)), per_expert,
                  pl.BlockSpec((D_MODEL // 4, eb), lambda i, e: (0, prev_block(e))), per_expert,
                  slab, slab, rowt, rowt],
        out_specs=pl.BlockSpec((D_MODEL, tm), lambda i, e: (0, i)),
        out_shape=jax.ShapeDtypeStruct((D_MODEL, t), F32),
        scratch_shapes=[pltpu.VMEM((eb // 2, tm), jnp.uint32),
                        pltpu.VMEM((2, eb // 4, tm), jnp.uint32)],
        compiler_params=_cparams("arbitrary", "arbitrary"),
        name="peer_apply",
    )(h2t, h2_scale, 1.0 / w_scale, u, u_scale, vt, v_scale, *tables)


def _final_kernel(x_ref, pt_ref, ws_ref, mod_ref, g_ref, b_ref, o_ref):
    peer = (pt_ref[...] * ws_ref[...]).T
    y = DN_ALPHA * x_ref[...] + mod_ref[5:6, :] * peer
    o_ref[...] = _ln(y) * g_ref[...] + b_ref[...]


def _finalize(x1, peer_t, w_scale, mod, mod_base, rows_per_mod, ln_g, ln_b):
    t = x1.shape[0]
    tiles_per_mod = rows_per_mod // ROW_TILE
    row = pl.BlockSpec((ROW_TILE, D_MODEL), lambda i: (i, 0))
    vec = pl.BlockSpec((1, D_MODEL), lambda i: (0, 0))
    return pl.pallas_call(
        _final_kernel,
        grid=(t // ROW_TILE,),
        in_specs=[row,
                  pl.BlockSpec((D_MODEL, ROW_TILE), lambda i: (0, i)),
                  pl.BlockSpec((1, ROW_TILE), lambda i: (0, i)),
                  pl.BlockSpec((None, N_MOD, D_MODEL), lambda i: (mod_base + i // tiles_per_mod, 0, 0)),
                  vec, vec],
        out_specs=row,
        out_shape=jax.ShapeDtypeStruct((t, D_MODEL), F32),
        compiler_params=_cparams("arbitrary"),
        name="finalize",
    )(x1, peer_t, w_scale, mod, ln_g, ln_b)


def _prep_layer(l, w_in, attn_sink, sgu_g, sgu_w, sgu_b, w_out, ln1_g, ln1_b,
                peer_wq, peer_keys, peer_u, peer_v, ln2_g, ln2_b):
    keys = peer_keys[l]
    zeros = jnp.zeros_like(keys[:, 0])
    kmat = jnp.concatenate([jnp.concatenate([keys[:, 0], zeros], axis=-1),
                            jnp.concatenate([zeros, keys[:, 1]], axis=-1)], axis=1)
    return dict(
        w_in=w_in[l].astype(BF16),
        sink_b=jnp.broadcast_to(
            jnp.pad(attn_sink[l].reshape(N_KV, GROUP), ((0, 0), (0, SUBLANES - GROUP)))[:, :, None],
            (N_KV, SUBLANES, LANES)),
        sgu_g=sgu_g[l].reshape(1, SGU_W),
        sgu_w=sgu_w[l].astype(BF16),
        sgu_bias=jnp.repeat(sgu_b[l].T, LANES, axis=1),
        w_out=w_out[l].astype(BF16),
        ln1_g=ln1_g[l].reshape(1, D_MODEL), ln1_b=ln1_b[l].reshape(1, D_MODEL),
        wq=peer_wq[l].astype(BF16),
        kmat=kmat.astype(BF16),
        u=_pack_table_fp8(peer_u, l, transpose=False),
        vt=_pack_table_fp8(peer_v, l, transpose=True),
        ln2_g=ln2_g[l].reshape(1, D_MODEL), ln2_b=ln2_b[l].reshape(1, D_MODEL),
    )


def _layer(x, batch, seq, mod, mod_base, rows_per_mod, lw, ctx):
    q, k, v, ug, vg, f = _in_projection(x, mod, mod_base, rows_per_mod, lw["w_in"])
    if ctx is None:
        attn = _context_attention(q, k, v, lw["sink_b"], batch, seq)
    else:
        cache_k, cache_v, layer, cos_t, sin_t = ctx
        attn = _latent_attention(q, k, v, cache_k, cache_v, layer, cos_t, sin_t, lw["sink_b"], batch, seq)
    sg = _spatial_gating(ug, vg, lw["sgu_g"], lw["sgu_w"], lw["sgu_bias"])
    fn = _fourier_mix(f, batch, seq)
    x1, h2t, h2_scale, h2_norm, qp = _out_projection(x, attn, sg, fn, lw["w_out"], mod, mod_base,
                                                     rows_per_mod, lw["ln1_g"], lw["ln1_b"], lw["wq"])
    tables = _peer_topk(qp, lw["kmat"])
    u, u_scale, u_norm = lw["u"]
    vt, v_scale, _ = lw["vt"]
    w_scale = _peer_weight_scale(h2_norm, u_norm, v_scale)
    peer_t = _peer_apply(h2t, h2_scale, w_scale, u, u_scale, vt, v_scale, tables)
    x2 = _finalize(x1, peer_t, w_scale, mod, mod_base, rows_per_mod, lw["ln2_g"], lw["ln2_b"])
    return x2, k, v


def kernel(x_prompt, x_sample, cache_k, cache_v, c, c_ctx, w_mod, b_mod, w_in, attn_sink, sgu_g, sgu_w, sgu_b, w_out, ln1_g, ln1_b, peer_wq, peer_keys, peer_u, peer_v, ln2_g, ln2_b):
    batch, seq, _ = x_prompt.shape
    dec_batch, dec_seq, _ = x_sample.shape
    past = cache_k.shape[2]
    assert 1 + dec_batch <= MOD_ROWS

    cvec = jnp.concatenate([c_ctx[None, :], c, jnp.zeros((MOD_ROWS - 1 - dec_batch, D_MODEL), F32)], axis=0)
    mod = _modulation(cvec, w_mod, b_mod)
    cos_t, sin_t = _rope_tables(dec_seq)
    ck = cache_k.reshape(dec_batch, DEPTH, past, KV_W)
    cv = cache_v.reshape(dec_batch, DEPTH, past, KV_W)

    xp = x_prompt.reshape(batch * seq, D_MODEL)
    xs = x_sample.reshape(dec_batch * dec_seq, D_MODEL)
    new_k, new_v = [], []
    for l in range(DEPTH):
        lw = _prep_layer(l, w_in, attn_sink, sgu_g, sgu_w, sgu_b, w_out, ln1_g, ln1_b,
                         peer_wq, peer_keys, peer_u, peer_v, ln2_g, ln2_b)
        xp, kl, vl = _layer(xp, batch, seq, mod[l], 0, batch * seq, lw, None)
        new_k.append(kl.reshape(batch, seq, N_KV, HEAD_DIM))
        new_v.append(vl.reshape(batch, seq, N_KV, HEAD_DIM))
        xs, _, _ = _layer(xs, dec_batch, dec_seq, mod[l], 1, dec_seq, lw, (ck, cv, l, cos_t, sin_t))
    return (xp.reshape(batch, seq, D_MODEL), xs.reshape(dec_batch, dec_seq, D_MODEL),
            jnp.stack(new_k, axis=1), jnp.stack(new_v, axis=1))
```

```python
import functools

import jax
import jax.numpy as jnp
from jax import lax
from jax.experimental import pallas as pl
from jax.experimental.pallas import tpu as pltpu

D_MODEL = 2048
DEPTH = 2
GRID_W = 64
HEAD_DIM = 128
N_HEADS = 8
N_KV = 2
GROUP = N_HEADS // N_KV
ATTN_W = N_HEADS * HEAD_DIM
KV_W = N_KV * HEAD_DIM
SGU_GROUPS = 4
SGU_W = 512
FNET_GROUPS = 4
FNET_W = 512
FNET_GD = 128
PROJ_W = ATTN_W + 2 * KV_W + 2 * SGU_W + FNET_W
BLOCK = 128
WINDOW = 128
CHUNK = 128
ROPE_BASE = 10000.0
N_MOD = 6
PEER_HEADS = 8
PEER_NKEYS = 128
PEER_N = PEER_NKEYS * PEER_NKEYS
PEER_TOPK = 16
PEER_QDIM = 128
PEER_HALF = PEER_QDIM // 2
DN_ALPHA = (2 * DEPTH) ** 0.25
LN_EPS = 1e-5
NEG_INF = -1e30

F32 = jnp.float32
BF16 = jnp.bfloat16
FP8 = jnp.float8_e4m3fn
FP8_TOP = 256.0
LANES = 128
SUBLANES = 8
V7X_VMEM_BYTES = 64 * 2 ** 20
VMEM_LIMIT = V7X_VMEM_BYTES - 8 * 2 ** 20
ROW_TILE = 256
TOPK_TOKEN_TILE = 512
PEER_TOKEN_TILE = 512
PEER_EXPERT_SUBTILE = 128
PEER_OUT_ROWS = 256
PEER_EXPERT_TILE = SUBLANES * PEER_NKEYS
MOD_COL_TILE = 1024
MOD_ROWS = 16

_NT = (((1,), (1,)), ((), ()))


def _cparams(*sem):
    return pltpu.CompilerParams(dimension_semantics=sem, vmem_limit_bytes=VMEM_LIMIT)


def _row_reduce(op, lane_op, *blocks):
    acc = None
    for x in blocks:
        for i in range(x.shape[1] // LANES):
            chunk = x[:, i * LANES:(i + 1) * LANES]
            acc = chunk if acc is None else op(acc, chunk)
    return lane_op(acc, axis=-1, keepdims=True)


def _ln(x):
    inv_n = 1.0 / x.shape[-1]
    xc = x - _row_reduce(jnp.add, jnp.sum, x) * inv_n
    var = _row_reduce(jnp.add, jnp.sum, xc * xc) * inv_n
    return xc * lax.rsqrt(var + LN_EPS)


def _gelu(x):
    return 0.5 * x * (1.0 + lax.erf(x * (0.5 ** 0.5)))


def _pack_bf16(x):
    return pltpu.bitcast(x.astype(BF16), jnp.uint32)


def _unpack_bf16(words):
    return pltpu.bitcast(words, BF16)


def _fp8_scale(amax):
    return jnp.where(amax > 0.0, amax, FP8_TOP) * (1.0 / FP8_TOP)


def _pack_fp8(x):
    return pltpu.bitcast(x.astype(FP8), jnp.uint32)


def _unpack_fp8(words):
    return pltpu.bitcast(words, FP8)


def _pack_fp8_cols(xt):
    scale = _fp8_scale(jnp.max(jnp.abs(xt), axis=0, keepdims=True))
    return _pack_fp8(xt / scale), scale, jnp.sqrt(jnp.sum(xt * xt, axis=0, keepdims=True))


def _pack_rows_kernel(x_ref, o_ref, s_ref, n_ref, *, transpose):
    x = x_ref[...]
    scale = _fp8_scale(_row_reduce(jnp.maximum, jnp.max, jnp.abs(x)))
    xs = x / scale
    o_ref[...] = _pack_fp8(xs.T if transpose else xs)
    s_ref[...] = jnp.broadcast_to(scale, s_ref.shape)
    n_ref[...] = jnp.broadcast_to(jnp.sqrt(_row_reduce(jnp.add, jnp.sum, x * x)), n_ref.shape)


def _pack_table_fp8(table, layer, transpose):
    _, n, d = table.shape
    rows = PEER_EXPERT_TILE
    if transpose:
        out_spec = pl.BlockSpec((d // 4, rows), lambda e: (0, e))
        out_shape = jax.ShapeDtypeStruct((d // 4, n), jnp.uint32)
    else:
        out_spec = pl.BlockSpec((rows // 4, d), lambda e: (e, 0))
        out_shape = jax.ShapeDtypeStruct((n // 4, d), jnp.uint32)
    per_row = pl.BlockSpec((rows, LANES), lambda e: (e, 0))
    per_row_shape = jax.ShapeDtypeStruct((n, LANES), F32)
    return pl.pallas_call(
        functools.partial(_pack_rows_kernel, transpose=transpose),
        grid=(n // rows,),
        in_specs=[pl.BlockSpec((None, rows, d), lambda e: (layer, e, 0))],
        out_specs=[out_spec, per_row, per_row],
        out_shape=[out_shape, per_row_shape, per_row_shape],
        compiler_params=_cparams("arbitrary"),
        name="pack_table_fp8_t" if transpose else "pack_table_fp8",
    )(table)


def _mod_kernel(c_ref, w_ref, b_ref, o_ref):
    c = c_ref[...]
    a = (c / (1.0 + jnp.exp(-c))).astype(BF16)
    o_ref[...] = jnp.dot(a, w_ref[...].astype(BF16), preferred_element_type=F32) + b_ref[...]


def _modulation(cvec, w_mod, b_mod):
    n = N_MOD * D_MODEL
    out = pl.pallas_call(
        _mod_kernel,
        grid=(DEPTH, n // MOD_COL_TILE),
        in_specs=[
            pl.BlockSpec((MOD_ROWS, D_MODEL), lambda l, j: (0, 0)),
            pl.BlockSpec((None, D_MODEL, MOD_COL_TILE), lambda l, j: (l, 0, j)),
            pl.BlockSpec((None, 1, MOD_COL_TILE), lambda l, j: (l, 0, j)),
        ],
        out_specs=pl.BlockSpec((None, MOD_ROWS, MOD_COL_TILE), lambda l, j: (l, 0, j)),
        out_shape=jax.ShapeDtypeStruct((DEPTH, MOD_ROWS, n), F32),
        compiler_params=_cparams("arbitrary", "arbitrary"),
        name="modulation",
    )(cvec, w_mod, b_mod.reshape(DEPTH, 1, n))
    return out.reshape(DEPTH, MOD_ROWS, N_MOD, D_MODEL)


_PROJ_SLICES = ((0, ATTN_W), (ATTN_W, KV_W), (ATTN_W + KV_W, KV_W),
                (ATTN_W + 2 * KV_W, SGU_W), (ATTN_W + 2 * KV_W + SGU_W, SGU_W),
                (ATTN_W + 2 * KV_W + 2 * SGU_W, FNET_W))


def _proj_kernel(x_ref, mod_ref, w_ref, g_ref, ws_ref, bias_ref, q_ref, k_ref, v_ref, sg_ref, f_ref):
    h = _ln(x_ref[...]) * (1.0 + mod_ref[1:2, :]) + mod_ref[0:1, :]
    h = h.astype(BF16)
    proj = [jnp.dot(h, w_ref[:, start:start + width], preferred_element_type=F32)
            for start, width in _PROJ_SLICES]
    q_ref[...], k_ref[...], v_ref[...], f_ref[...] = proj[0], proj[1], proj[2], proj[5]
    _spatial_gate(proj[3], proj[4], g_ref, ws_ref, bias_ref, sg_ref)


def _in_projection(x, mod, mod_base, rows_per_mod, w_in, sgu_g, sgu_w, sgu_bias):
    t = x.shape[0]
    tiles_per_mod = rows_per_mod // ROW_TILE
    row = lambda w: pl.BlockSpec((ROW_TILE, w), lambda i: (i, 0))
    widths = (ATTN_W, KV_W, KV_W, SGU_W, FNET_W)
    return pl.pallas_call(
        _proj_kernel,
        grid=(t // ROW_TILE,),
        in_specs=[
            row(D_MODEL),
            pl.BlockSpec((None, N_MOD, D_MODEL), lambda i: (mod_base + i // tiles_per_mod, 0, 0)),
            pl.BlockSpec((D_MODEL, PROJ_W), lambda i: (0, 0)),
            pl.BlockSpec((1, SGU_W), lambda i: (0, 0)),
            pl.BlockSpec((SGU_GROUPS, CHUNK, CHUNK), lambda i: (0, 0, 0)),
            pl.BlockSpec((CHUNK, SGU_W), lambda i: (0, 0)),
        ],
        out_specs=[row(w) for w in widths],
        out_shape=[jax.ShapeDtypeStruct((t, w), F32) for w in widths],
        compiler_params=_cparams("arbitrary"),
        name="in_projection",
    )(x, mod, w_in, sgu_g, sgu_w, sgu_bias)


def _head_cols(g):
    return slice(g * HEAD_DIM, (g + 1) * HEAD_DIM)


def _stacked_sinks(sink_ref, rows):
    return jnp.concatenate([jnp.broadcast_to(sink_ref[g:g + 1, 0:1], (rows, 1)) for g in range(GROUP)],
                           axis=0)


def _ctx_attn_kernel(q_ref, k_ref, v_ref, sink_ref, o_ref):
    seq = q_ref.shape[0]
    k = k_ref[...].astype(BF16)
    v = v_ref[...].astype(BF16)
    q = jnp.concatenate([q_ref[:, _head_cols(g)] for g in range(GROUP)], axis=0).astype(BF16)
    s = lax.dot_general(q, k, _NT, preferred_element_type=F32) * (HEAD_DIM ** -0.5)
    sk = _stacked_sinks(sink_ref, seq)
    m = jnp.maximum(_row_reduce(jnp.maximum, jnp.max, s), sk)
    p = jnp.exp(s - m)
    den = _row_reduce(jnp.add, jnp.sum, p) + jnp.exp(sk - m)
    o = jnp.dot(p.astype(BF16), v, preferred_element_type=F32) / den
    for g in range(GROUP):
        o_ref[:, _head_cols(g)] = o[g * seq:(g + 1) * seq]


def _context_attention(q, k, v, sink_b, batch, seq):
    t = q.shape[0]
    qspec = pl.BlockSpec((seq, GROUP * HEAD_DIM), lambda b, h: (b, h))
    kvspec = pl.BlockSpec((seq, HEAD_DIM), lambda b, h: (b, h))
    return pl.pallas_call(
        _ctx_attn_kernel,
        grid=(batch, N_KV),
        in_specs=[qspec, kvspec, kvspec, pl.BlockSpec((None, SUBLANES, LANES), lambda b, h: (h, 0, 0))],
        out_specs=qspec,
        out_shape=jax.ShapeDtypeStruct((t, ATTN_W), F32),
        compiler_params=_cparams("arbitrary", "arbitrary"),
        name="context_attention",
    )(q, k, v, sink_b)


def _lat_attn_kernel(q_ref, k_ref, v_ref, ck_ref, cv_ref, cos_ref, sin_ref, sink_ref, o_ref,
                     kp_ref, vp_ref, *, seq):
    scale = HEAD_DIM ** -0.5
    lane = lax.broadcasted_iota(jnp.int32, (1, HEAD_DIM), 1)
    first = (lane % 64) < 32

    def rope(x, cos, sin):
        sw = jnp.where(first, pltpu.roll(x, 96, 1), pltpu.roll(x, 32, 1))
        return x * cos + sw * sin

    pad = jnp.zeros((BLOCK, HEAD_DIM), BF16)
    kp_ref[0:BLOCK, :] = pad
    kp_ref[seq + BLOCK:seq + 2 * BLOCK, :] = pad
    vp_ref[0:BLOCK, :] = pad
    vp_ref[seq + BLOCK:seq + 2 * BLOCK, :] = pad
    kp_ref[BLOCK:seq + BLOCK, :] = rope(k_ref[...], cos_ref[...], sin_ref[...]).astype(BF16)
    vp_ref[BLOCK:seq + BLOCK, :] = v_ref[...].astype(BF16)
    ck = ck_ref[...].astype(BF16)
    cv = cv_ref[...].astype(BF16)
    sk = _stacked_sinks(sink_ref, BLOCK)

    def body(qb, carry):
        r0 = pl.multiple_of(qb * BLOCK, BLOCK)
        kb = kp_ref[pl.ds(r0, 3 * BLOCK), :]
        vb = vp_ref[pl.ds(r0, 3 * BLOCK), :]
        cosq = cos_ref[pl.ds(r0, BLOCK), :]
        sinq = sin_ref[pl.ds(r0, BLOCK), :]
        r = lax.broadcasted_iota(jnp.int32, (GROUP * BLOCK, 3 * BLOCK), 0) & (BLOCK - 1)
        j = lax.broadcasted_iota(jnp.int32, (GROUP * BLOCK, 3 * BLOCK), 1)
        kpos = j + (r0 - BLOCK)
        valid = (j >= r) & (j <= r + 2 * WINDOW) & (kpos >= 0) & (kpos < seq)
        q = jnp.concatenate([rope(q_ref[pl.ds(r0, BLOCK), _head_cols(g)], cosq, sinq)
                             for g in range(GROUP)], axis=0).astype(BF16)
        sl = lax.dot_general(q, kb, _NT, preferred_element_type=F32) * scale
        sl = jnp.where(valid, sl, NEG_INF)
        sc = lax.dot_general(q, ck, _NT, preferred_element_type=F32) * scale
        m = jnp.maximum(_row_reduce(jnp.maximum, jnp.max, sl, sc), sk)
        p_l = jnp.exp(sl - m)
        p_c = jnp.exp(sc - m)
        den = _row_reduce(jnp.add, jnp.sum, p_l, p_c) + jnp.exp(sk - m)
        o = (jnp.dot(p_l.astype(BF16), vb, preferred_element_type=F32)
             + jnp.dot(p_c.astype(BF16), cv, preferred_element_type=F32)) / den
        for g in range(GROUP):
            o_ref[pl.ds(r0, BLOCK), _head_cols(g)] = o[g * BLOCK:(g + 1) * BLOCK]
        return carry

    lax.fori_loop(0, seq // BLOCK, body, 0, unroll=4)


def _latent_attention(q, k, v, cache_k, cache_v, layer, cos_t, sin_t, sink_b, batch, seq):
    t = q.shape[0]
    past = cache_k.shape[2]
    qspec = pl.BlockSpec((seq, GROUP * HEAD_DIM), lambda b, h: (b, h))
    kvspec = pl.BlockSpec((seq, HEAD_DIM), lambda b, h: (b, h))
    cspec = pl.BlockSpec((None, None, past, HEAD_DIM), lambda b, h: (b, layer, 0, h))
    tspec = pl.BlockSpec((seq, HEAD_DIM), lambda b, h: (0, 0))
    return pl.pallas_call(
        functools.partial(_lat_attn_kernel, seq=seq),
        grid=(batch, N_KV),
        in_specs=[qspec, kvspec, kvspec, cspec, cspec, tspec, tspec,
                  pl.BlockSpec((None, SUBLANES, LANES), lambda b, h: (h, 0, 0))],
        out_specs=qspec,
        out_shape=jax.ShapeDtypeStruct((t, ATTN_W), F32),
        scratch_shapes=[pltpu.VMEM((seq + 2 * BLOCK, HEAD_DIM), BF16),
                        pltpu.VMEM((seq + 2 * BLOCK, HEAD_DIM), BF16)],
        compiler_params=_cparams("arbitrary", "arbitrary"),
        name="latent_attention",
    )(q, k, v, cache_k, cache_v, cos_t, sin_t, sink_b)


def _rope_tables(seq):
    pos = jnp.arange(seq)
    r = (pos // GRID_W).astype(F32)
    col = (pos % GRID_W).astype(F32)
    n = HEAD_DIM // 4
    inv = ROPE_BASE ** (-jnp.arange(n, dtype=F32) / n)
    ang_r = r[:, None] * inv
    ang_c = col[:, None] * inv
    cos_t = jnp.concatenate([jnp.cos(ang_r), jnp.cos(ang_r), jnp.cos(ang_c), jnp.cos(ang_c)], axis=-1)
    sin_t = jnp.concatenate([-jnp.sin(ang_r), jnp.sin(ang_r), -jnp.sin(ang_c), jnp.sin(ang_c)], axis=-1)
    return cos_t, sin_t


def _spatial_gate(ug, vg, g_ref, ws_ref, bias_ref, o_ref):
    for c in range(ROW_TILE // CHUNK):
        rows = slice(c * CHUNK, (c + 1) * CHUNK)
        for g in range(SGU_GROUPS):
            cols = slice(g * LANES, (g + 1) * LANES)
            u = _gelu(ug[rows, cols])
            vn = _ln(_gelu(vg[rows, cols])) * g_ref[0:1, cols]
            mixed = jnp.dot(ws_ref[g], vn.astype(BF16), preferred_element_type=F32) + bias_ref[:, cols]
            o_ref[rows, cols] = u * mixed


def _fnet_kernel(f_ref, cc_ref, cs_ref, o_ref, z_ref, *, seq):
    @pl.when(pl.program_id(1) == 0)
    def _():
        cc = cc_ref[...]
        for g in range(FNET_GROUPS):
            cols = slice(g * FNET_GD, (g + 1) * FNET_GD)
            zz = jnp.dot(f_ref[:, cols].astype(BF16), cc, preferred_element_type=F32)
            z_ref[0:seq, cols] = zz[:, :FNET_GD].astype(BF16)
            z_ref[seq:2 * seq, cols] = zz[:, FNET_GD:].astype(BF16)

    y = jnp.dot(cs_ref[...], z_ref[...], preferred_element_type=F32)
    o_ref[...] = y * ((seq * FNET_GD) ** -0.5)


def _dft_tables(n):
    k = jnp.arange(n, dtype=jnp.int32)
    ang = ((k[:, None] * k[None, :]) % n).astype(F32) * (2.0 * jnp.pi / n)
    return jnp.cos(ang), jnp.sin(ang)


def _fourier_mix(f, batch, seq):
    t = f.shape[0]
    tr = min(seq, 512)
    c_c, s_c = _dft_tables(FNET_GD)
    c_s, s_s = _dft_tables(seq)
    cc = jnp.concatenate([c_c, s_c], axis=1).astype(BF16)
    cs = jnp.concatenate([c_s, -s_s], axis=1).astype(BF16)
    nj = seq // tr
    return pl.pallas_call(
        functools.partial(_fnet_kernel, seq=seq),
        grid=(batch, nj),
        in_specs=[pl.BlockSpec((seq, FNET_W), lambda b, j: (b, 0)),
                  pl.BlockSpec((FNET_GD, 2 * FNET_GD), lambda b, j: (0, 0)),
                  pl.BlockSpec((tr, 2 * seq), lambda b, j: (j, 0))],
        out_specs=pl.BlockSpec((tr, FNET_W), lambda b, j: (b * nj + j, 0)),
        out_shape=jax.ShapeDtypeStruct((t, FNET_W), F32),
        scratch_shapes=[pltpu.VMEM((2 * seq, FNET_W), BF16)],
        compiler_params=_cparams("arbitrary", "arbitrary"),
        name="fourier_mix",
    )(f, cc, cs)


def _mix_kernel(x_ref, a_ref, s_ref, f_ref, wo_ref, mod_ref, g1_ref, b1_ref, wq_ref,
                x1_ref, h2_ref, hs_ref, hn_ref, qp_ref):
    mix = jnp.dot(a_ref[...].astype(BF16), wo_ref[0:ATTN_W, :], preferred_element_type=F32)
    mix += jnp.dot(s_ref[...].astype(BF16), wo_ref[ATTN_W:ATTN_W + SGU_W, :], preferred_element_type=F32)
    mix += jnp.dot(f_ref[...].astype(BF16), wo_ref[ATTN_W + SGU_W:, :], preferred_element_type=F32)
    x1 = _ln(DN_ALPHA * x_ref[...] + mod_ref[2:3, :] * mix) * g1_ref[...] + b1_ref[...]
    x1_ref[...] = x1
    h2 = _ln(x1) * (1.0 + mod_ref[4:5, :]) + mod_ref[3:4, :]
    h2_ref[...], hs_ref[...], hn_ref[...] = _pack_fp8_cols(h2.T)
    q = jnp.dot(h2.astype(BF16), wq_ref[...], preferred_element_type=F32)
    for h in range(PEER_HEADS):
        qp_ref[h] = q[:, h * PEER_QDIM:(h + 1) * PEER_QDIM].astype(BF16)


def _out_projection(x, attn, sg, fn, w_out, mod, mod_base, rows_per_mod, ln_g, ln_b, wq):
    t = x.shape[0]
    tiles_per_mod = rows_per_mod // ROW_TILE
    row = lambda w: pl.BlockSpec((ROW_TILE, w), lambda i: (i, 0))
    full = lambda a, b: pl.BlockSpec((a, b), lambda i: (0, 0))
    return pl.pallas_call(
        _mix_kernel,
        grid=(t // ROW_TILE,),
        in_specs=[row(D_MODEL), row(ATTN_W), row(SGU_W), row(FNET_W),
                  full(D_MODEL, D_MODEL),
                  pl.BlockSpec((None, N_MOD, D_MODEL), lambda i: (mod_base + i // tiles_per_mod, 0, 0)),
                  full(1, D_MODEL), full(1, D_MODEL),
                  full(D_MODEL, PEER_HEADS * PEER_QDIM)],
        out_specs=[row(D_MODEL), pl.BlockSpec((D_MODEL // 4, ROW_TILE), lambda i: (0, i)),
                   pl.BlockSpec((1, ROW_TILE), lambda i: (0, i)),
                   pl.BlockSpec((1, ROW_TILE), lambda i: (0, i)),
                   pl.BlockSpec((PEER_HEADS, ROW_TILE, PEER_QDIM), lambda i: (0, i, 0))],
        out_shape=[jax.ShapeDtypeStruct((t, D_MODEL), F32),
                   jax.ShapeDtypeStruct((D_MODEL // 4, t), jnp.uint32),
                   jax.ShapeDtypeStruct((1, t), F32),
                   jax.ShapeDtypeStruct((1, t), F32),
                   jax.ShapeDtypeStruct((PEER_HEADS, t, PEER_QDIM), BF16)],
        compiler_params=_cparams("arbitrary"),
        name="out_projection",
    )(x, attn, sg, fn, w_out, mod, ln_g, ln_b, wq)


def _pair_candidates(v1, v2):
    half = PEER_TOPK // 2
    pieces = [v1[0:1] + v2] + [v1[r1:r1 + 1] + v2[0:half] for r1 in range(1, PEER_TOPK)]
    pos_pieces = [lax.broadcasted_iota(jnp.int32, (PEER_TOPK, LANES), 0)]
    for r1 in range(1, PEER_TOPK):
        pos_pieces.append(lax.broadcasted_iota(jnp.int32, (half, LANES), 0) + r1 * PEER_TOPK)
    spans = [(0, PEER_TOPK)] + [(PEER_TOPK + (r1 - 1) * half, half) for r1 in range(1, PEER_TOPK)]
    return jnp.concatenate(pieces, axis=0), jnp.concatenate(pos_pieces, axis=0).astype(F32), spans


def _descending_maxima(slabs):
    w = slabs[0].shape[1]
    slot = lax.broadcasted_iota(jnp.int32, (PEER_TOPK, w), 0)

    def body(k, carry):
        out = []
        for s, (prev, vals) in zip(slabs, carry):
            m = jnp.max(jnp.where(s < prev, s, -jnp.inf), axis=0, keepdims=True)
            out.append((m, jnp.where(slot == k, m, vals)))
        return tuple(out)

    init = tuple((jnp.full((1, w), jnp.inf, F32), jnp.zeros((PEER_TOPK, w), F32)) for _ in slabs)
    return [vals for _, vals in lax.fori_loop(0, PEER_TOPK, body, init)]


def _count_rows(mask):
    return jnp.sum(jnp.where(mask, 1.0, 0.0), axis=0, keepdims=True)


def _topk_tables_distinct(s1, s2):
    nc = len(s1)
    k = float(PEER_TOPK)
    vals = _descending_maxima(s1 + s2)
    v1, v2 = vals[:nc], vals[nc:]
    cands = [_pair_candidates(a, b) for a, b in zip(v1, v2)]
    best = _descending_maxima([cand for cand, _, _ in cands])
    tables, bad = [], jnp.zeros((1, LANES), F32)
    for c in range(nc):
        cand, _, spans = cands[c]
        taken = jnp.where(cand >= best[c][PEER_TOPK - 1:PEER_TOPK], 1.0, 0.0)
        cnt = [jnp.sum(taken[lo:lo + n], axis=0, keepdims=True) for lo, n in spans]
        z = jnp.sum(jnp.exp(best[c] - best[c][0:1]), axis=0, keepdims=True)
        rank2 = jnp.zeros((PEER_NKEYS, LANES), F32)
        n1 = jnp.zeros((PEER_NKEYS, LANES), F32)
        for r in range(PEER_TOPK):
            rank2 = rank2 + jnp.where(v2[c][r:r + 1] > s2[c], 1.0, 0.0)
            n1 = jnp.where(s1[c] == v1[c][r:r + 1], cnt[r], n1)
        tables.append((rank2, jnp.exp(s2[c] - v2[c][0:1]), n1, jnp.exp(s1[c] - v1[c][0:1]) / z))
        for s, v in ((s1[c], v1[c]), (s2[c], v2[c])):
            bad = bad + jnp.abs(_count_rows(s >= v[PEER_TOPK - 1:PEER_TOPK]) - k)
        bad = bad + jnp.abs(sum(cnt) - k)
    return tables, bad


def _top16_rows(scores):
    n, w = scores[0].shape
    row = lax.broadcasted_iota(jnp.int32, (n, w), 0).astype(F32)
    slot = lax.broadcasted_iota(jnp.int32, (PEER_TOPK, w), 0)

    def body(k, carry):
        kf = lax.convert_element_type(k, F32)
        out = []
        for cur, rank, vals in carry:
            m = jnp.max(cur, axis=0, keepdims=True)
            hit = row == jnp.min(jnp.where(cur == m, row, float(n)), axis=0, keepdims=True)
            out.append((jnp.where(hit, -jnp.inf, cur), jnp.where(hit, kf, rank),
                        jnp.where(slot == k, m, vals)))
        return tuple(out)

    init = tuple((s, jnp.full((n, w), float(PEER_TOPK), F32), jnp.zeros((PEER_TOPK, w), F32))
                 for s in scores)
    res = lax.fori_loop(0, PEER_TOPK, body, init)
    return [(rank, vals) for _, rank, vals in res]


def _topk_tables_exact(s1, s2):
    nc = len(s1)
    tops = _top16_rows(s1 + s2)
    slot = lax.broadcasted_iota(jnp.int32, (PEER_TOPK, LANES), 0).astype(F32)
    cands = [_pair_candidates(tops[c][1], tops[nc + c][1]) for c in range(nc)]

    def body(k, carry):
        out = []
        for (cand, cnt, z), (cand0, pos, _) in zip(carry, cands):
            m = jnp.max(cand, axis=0, keepdims=True)
            p = jnp.min(jnp.where(cand == m, pos, 4096.0), axis=0, keepdims=True)
            out.append((jnp.where(pos == p, -jnp.inf, cand),
                        cnt + jnp.where(slot == jnp.floor(p * (1.0 / PEER_TOPK)), 1.0, 0.0),
                        z + jnp.exp(m - cand0[0:1])))
        return tuple(out)

    init = tuple((cand, jnp.zeros((PEER_TOPK, LANES), F32), jnp.zeros((1, LANES), F32))
                 for cand, _, _ in cands)
    res = lax.fori_loop(0, PEER_TOPK, body, init)
    tables = []
    for c in range(nc):
        (rank1, v1), (rank2, v2) = tops[c], tops[nc + c]
        _, cnt, z = res[c]
        n1 = jnp.zeros((PEER_NKEYS, LANES), F32)
        for r1 in range(PEER_TOPK):
            n1 = jnp.where(rank1 == float(r1), cnt[r1:r1 + 1], n1)
        tables.append((rank2, jnp.exp(s2[c] - v2[0:1]), n1, jnp.exp(s1[c] - v1[0:1]) / z))
    return tables


def _topk_kernel(q_ref, km_ref, r2_ref, e2_ref, n1_ref, e1_ref, *, tm):
    km = km_ref[...]
    toks = [slice(c * LANES, (c + 1) * LANES) for c in range(tm // LANES)]
    sts = [lax.dot_general(km, q_ref[tok, :], _NT, preferred_element_type=F32) for tok in toks]
    s1 = [st[0:PEER_NKEYS] for st in sts]
    s2 = [st[PEER_NKEYS:2 * PEER_NKEYS] for st in sts]

    def store(tables):
        for tok, (rank2, e2, n1, e1) in zip(toks, tables):
            r2_ref[:, tok] = _pack_bf16(rank2)
            e2_ref[:, tok] = _pack_bf16(e2)
            n1_ref[:, tok] = n1
            e1_ref[:, tok] = e1

    tables, bad = _topk_tables_distinct(s1, s2)
    store(tables)

    @pl.when(jnp.max(bad) > 0.0)
    def _():
        store(_topk_tables_exact(s1, s2))


def _peer_topk(qp, kmat):
    t = qp.shape[1]
    tm = min(TOPK_TOKEN_TILE, t)
    slab_spec = pl.BlockSpec((PEER_NKEYS // 2, tm), lambda i, h: (h, i))
    slab_shape = jax.ShapeDtypeStruct((PEER_HEADS * PEER_NKEYS // 2, t), jnp.uint32)
    row_spec = pl.BlockSpec((PEER_NKEYS, tm), lambda i, h: (h, i))
    row_shape = jax.ShapeDtypeStruct((PEER_HEADS * PEER_NKEYS, t), F32)
    return pl.pallas_call(
        functools.partial(_topk_kernel, tm=tm),
        grid=(t // tm, PEER_HEADS),
        in_specs=[pl.BlockSpec((None, tm, PEER_QDIM), lambda i, h: (h, i, 0)),
                  pl.BlockSpec((None, 2 * PEER_NKEYS, PEER_QDIM), lambda i, h: (h, 0, 0))],
        out_specs=[slab_spec, slab_spec, row_spec, row_spec],
        out_shape=[slab_shape, slab_shape, row_shape, row_shape],
        compiler_params=_cparams("arbitrary", "arbitrary"),
        name="peer_topk",
    )(qp, kmat)


def _peer_kernel(h_ref, hs_ref, iw_ref, u_ref, us_ref, vt_ref, vs_ref, r2_ref, e2_ref, n1_ref, e1_ref,
                 o_ref, g_ref, w_ref, *, tm, n_steps):
    e = pl.program_id(1)

    @pl.when(e == 0)
    def _():
        o_ref[...] = jnp.zeros_like(o_ref)
        _peer_gated_activations(e, 0, h_ref, hs_ref, iw_ref, u_ref, us_ref, vs_ref, r2_ref, e2_ref,
                                n1_ref, e1_ref, g_ref, w_ref, tm)

    @pl.when((e > 0) & (e < n_steps))
    def _():
        chunks = _peer_accumulate_chunks((e - 1) % 2, vt_ref, w_ref, o_ref)
        _peer_gated_activations(e, e % 2, h_ref, hs_ref, iw_ref, u_ref, us_ref, vs_ref, r2_ref, e2_ref,
                                n1_ref, e1_ref, g_ref, w_ref, tm, between=chunks)

    @pl.when(e == n_steps)
    def _():
        _peer_accumulate((n_steps - 1) % 2, vt_ref, w_ref, o_ref)


def _peer_accumulate_chunks(slot, vt_ref, w_ref, o_ref):
    def chunk(r0):
        rows = slice(r0, r0 + PEER_OUT_ROWS)
        o_ref[rows, :] += jnp.dot(_unpack_fp8(vt_ref[r0 // 4:(r0 + PEER_OUT_ROWS) // 4, :]),
                                  _unpack_fp8(w_ref[slot]), preferred_element_type=F32)
    return [functools.partial(chunk, r0) for r0 in range(0, D_MODEL, PEER_OUT_ROWS)]


def _peer_accumulate(slot, vt_ref, w_ref, o_ref):
    for chunk in _peer_accumulate_chunks(slot, vt_ref, w_ref, o_ref):
        chunk()


def _peer_gated_activations(e, slot, h_ref, hs_ref, iw_ref, u_ref, us_ref, vs_ref, r2_ref, e2_ref,
                            n1_ref, e1_ref, g_ref, w_ref, tm, between=()):
    between = list(between)
    assert len(between) in (0, SUBLANES)
    words = PEER_NKEYS // 2
    quads = PEER_NKEYS // 4
    pack = 2 * SUBLANES
    groups = PEER_NKEYS // pack

    for bi in range(SUBLANES):
        experts = slice(bi * PEER_NKEYS, (bi + 1) * PEER_NKEYS)
        for c in range(tm // LANES):
            tok = slice(c * LANES, (c + 1) * LANES)
            gate = jnp.zeros((groups, pack, LANES), BF16)
            for h in range(PEER_HEADS):
                keys = slice(h * words, (h + 1) * words)
                grp = pl.ds(pl.multiple_of(h * PEER_NKEYS + e * SUBLANES, SUBLANES), SUBLANES)
                n1 = jnp.broadcast_to(n1_ref[grp, tok][bi:bi + 1, :], (pack, LANES)).astype(BF16)
                e1 = jnp.broadcast_to(e1_ref[grp, tok][bi:bi + 1, :], (pack, LANES)).astype(BF16)
                r2 = _unpack_bf16(r2_ref[keys, tok]).reshape(groups, pack, LANES)
                e2 = _unpack_bf16(e2_ref[keys, tok]).reshape(groups, pack, LANES)
                gate += jnp.where(r2 < n1[None], e2, 0.0) * e1[None]
            rescale = (vs_ref[experts, :] * iw_ref[:, tok]).astype(BF16)
            g_ref[bi * words:(bi + 1) * words, tok] = pltpu.bitcast(
                gate.reshape(PEER_NKEYS, LANES) * rescale, jnp.uint32)
        if between:
            between[bi]()

    ht = _unpack_fp8(h_ref[...])
    per_sub = PEER_EXPERT_SUBTILE // PEER_NKEYS
    for sb in range(PEER_EXPERT_TILE // PEER_EXPERT_SUBTILE):
        urows = slice(sb * PEER_EXPERT_SUBTILE // 4, (sb + 1) * PEER_EXPERT_SUBTILE // 4)
        at = jnp.dot(_unpack_fp8(u_ref[urows, :]), ht, preferred_element_type=F32)
        for bs in range(per_sub):
            bi = sb * per_sub + bs
            half_us = 0.5 * us_ref[bi * PEER_NKEYS:(bi + 1) * PEER_NKEYS, :]
            for c in range(tm // LANES):
                tok = slice(c * LANES, (c + 1) * LANES)
                y = at[bs * PEER_NKEYS:(bs + 1) * PEER_NKEYS, tok] * half_us * hs_ref[:, tok]
                act = (y * (1.0 + lax.erf(y * (2.0 ** 0.5)))).astype(BF16)
                gate = _unpack_bf16(g_ref[bi * words:(bi + 1) * words, tok])
                w_ref[slot, bi * quads:(bi + 1) * quads, tok] = pltpu.bitcast((act * gate).astype(FP8),
                                                                                jnp.uint32)


def _peer_weight_scale(h2_norm, u_norm, v_scale):
    bound = PEER_HEADS * jnp.max(u_norm[:, 0] * v_scale[:, 0]) * h2_norm
    return jnp.where(bound > 0.0, bound, FP8_TOP) * (1.0 / FP8_TOP)


def _peer_apply(h2t, h2_scale, w_scale, u, u_scale, vt, v_scale, tables):
    t = h2t.shape[1]
    tm = min(PEER_TOKEN_TILE, t)
    eb = PEER_EXPERT_TILE
    slab = pl.BlockSpec((PEER_HEADS * PEER_NKEYS // 2, tm), lambda i, e: (0, i))
    rowt = pl.BlockSpec((PEER_HEADS * PEER_NKEYS, tm), lambda i, e: (0, i))
    n_steps = PEER_N // eb
    per_token = pl.BlockSpec((1, tm), lambda i, e: (0, i))
    this_block = lambda e: jnp.minimum(e, n_steps - 1)
    prev_block = lambda e: jnp.maximum(e - 1, 0)
    per_expert = pl.BlockSpec((eb, LANES), lambda i, e: (this_block(e), 0))
    return pl.pallas_call(
        functools.partial(_peer_kernel, tm=tm, n_steps=n_steps),
        grid=(t // tm, n_steps + 1),
        in_specs=[pl.BlockSpec((D_MODEL // 4, tm), lambda i, e: (0, i)), per_token, per_token,
                  pl.BlockSpec((eb // 4, D_MODEL), lambda i, e: (this_block(e), 0)), per_expert,
                  pl.BlockSpec((D_MODEL // 4, eb), lambda i, e: (0, prev_block(e))), per_expert,
                  slab, slab, rowt, rowt],
        out_specs=pl.BlockSpec((D_MODEL, tm), lambda i, e: (0, i)),
        out_shape=jax.ShapeDtypeStruct((D_MODEL, t), F32),
        scratch_shapes=[pltpu.VMEM((eb // 2, tm), jnp.uint32),
                        pltpu.VMEM((2, eb // 4, tm), jnp.uint32)],
        compiler_params=_cparams("arbitrary", "arbitrary"),
        name="peer_apply",
    )(h2t, h2_scale, 1.0 / w_scale, u, u_scale, vt, v_scale, *tables)


def _final_kernel(x_ref, pt_ref, ws_ref, mod_ref, g_ref, b_ref, o_ref):
    peer = (pt_ref[...] * ws_ref[...]).T
    y = DN_ALPHA * x_ref[...] + mod_ref[5:6, :] * peer
    o_ref[...] = _ln(y) * g_ref[...] + b_ref[...]


def _finalize(x1, peer_t, w_scale, mod, mod_base, rows_per_mod, ln_g, ln_b):
    t = x1.shape[0]
    tiles_per_mod = rows_per_mod // ROW_TILE
    row = pl.BlockSpec((ROW_TILE, D_MODEL), lambda i: (i, 0))
    vec = pl.BlockSpec((1, D_MODEL), lambda i: (0, 0))
    return pl.pallas_call(
        _final_kernel,
        grid=(t // ROW_TILE,),
        in_specs=[row,
                  pl.BlockSpec((D_MODEL, ROW_TILE), lambda i: (0, i)),
                  pl.BlockSpec((1, ROW_TILE), lambda i: (0, i)),
                  pl.BlockSpec((None, N_MOD, D_MODEL), lambda i: (mod_base + i // tiles_per_mod, 0, 0)),
                  vec, vec],
        out_specs=row,
        out_shape=jax.ShapeDtypeStruct((t, D_MODEL), F32),
        compiler_params=_cparams("arbitrary"),
        name="finalize",
    )(x1, peer_t, w_scale, mod, ln_g, ln_b)


def _prep_layer(l, w_in, attn_sink, sgu_g, sgu_w, sgu_b, w_out, ln1_g, ln1_b,
                peer_wq, peer_keys, peer_u, peer_v, ln2_g, ln2_b):
    keys = peer_keys[l]
    zeros = jnp.zeros_like(keys[:, 0])
    kmat = jnp.concatenate([jnp.concatenate([keys[:, 0], zeros], axis=-1),
                            jnp.concatenate([zeros, keys[:, 1]], axis=-1)], axis=1)
    return dict(
        w_in=w_in[l].astype(BF16),
        sink_b=jnp.broadcast_to(
            jnp.pad(attn_sink[l].reshape(N_KV, GROUP), ((0, 0), (0, SUBLANES - GROUP)))[:, :, None],
            (N_KV, SUBLANES, LANES)),
        sgu_g=sgu_g[l].reshape(1, SGU_W),
        sgu_w=sgu_w[l].astype(BF16),
        sgu_bias=jnp.repeat(sgu_b[l].T, LANES, axis=1),
        w_out=w_out[l].astype(BF16),
        ln1_g=ln1_g[l].reshape(1, D_MODEL), ln1_b=ln1_b[l].reshape(1, D_MODEL),
        wq=peer_wq[l].astype(BF16),
        kmat=kmat.astype(BF16),
        u=_pack_table_fp8(peer_u, l, transpose=False),
        vt=_pack_table_fp8(peer_v, l, transpose=True),
        ln2_g=ln2_g[l].reshape(1, D_MODEL), ln2_b=ln2_b[l].reshape(1, D_MODEL),
    )


def _layer(x, batch, seq, mod, mod_base, rows_per_mod, lw, ctx):
    q, k, v, sg, f = _in_projection(x, mod, mod_base, rows_per_mod, lw["w_in"],
                                    lw["sgu_g"], lw["sgu_w"], lw["sgu_bias"])
    if ctx is None:
        attn = _context_attention(q, k, v, lw["sink_b"], batch, seq)
    else:
        cache_k, cache_v, layer, cos_t, sin_t = ctx
        attn = _latent_attention(q, k, v, cache_k, cache_v, layer, cos_t, sin_t, lw["sink_b"], batch, seq)
    fn = _fourier_mix(f, batch, seq)
    x1, h2t, h2_scale, h2_norm, qp = _out_projection(x, attn, sg, fn, lw["w_out"], mod, mod_base,
                                                     rows_per_mod, lw["ln1_g"], lw["ln1_b"], lw["wq"])
    tables = _peer_topk(qp, lw["kmat"])
    u, u_scale, u_norm = lw["u"]
    vt, v_scale, _ = lw["vt"]
    w_scale = _peer_weight_scale(h2_norm, u_norm, v_scale)
    peer_t = _peer_apply(h2t, h2_scale, w_scale, u, u_scale, vt, v_scale, tables)
    x2 = _finalize(x1, peer_t, w_scale, mod, mod_base, rows_per_mod, lw["ln2_g"], lw["ln2_b"])
    return x2, k, v


def kernel(x_prompt, x_sample, cache_k, cache_v, c, c_ctx, w_mod, b_mod, w_in, attn_sink, sgu_g, sgu_w, sgu_b, w_out, ln1_g, ln1_b, peer_wq, peer_keys, peer_u, peer_v, ln2_g, ln2_b):
    batch, seq, _ = x_prompt.shape
    dec_batch, dec_seq, _ = x_sample.shape
    past = cache_k.shape[2]
    assert 1 + dec_batch <= MOD_ROWS

    cvec = jnp.concatenate([c_ctx[None, :], c, jnp.zeros((MOD_ROWS - 1 - dec_batch, D_MODEL), F32)], axis=0)
    mod = _modulation(cvec, w_mod, b_mod)
    cos_t, sin_t = _rope_tables(dec_seq)
    ck = cache_k.reshape(dec_batch, DEPTH, past, KV_W)
    cv = cache_v.reshape(dec_batch, DEPTH, past, KV_W)

    xp = x_prompt.reshape(batch * seq, D_MODEL)
    xs = x_sample.reshape(dec_batch * dec_seq, D_MODEL)
    new_k, new_v = [], []
    for l in range(DEPTH):
        lw = _prep_layer(l, w_in, attn_sink, sgu_g, sgu_w, sgu_b, w_out, ln1_g, ln1_b,
                         peer_wq, peer_keys, peer_u, peer_v, ln2_g, ln2_b)
        xp, kl, vl = _layer(xp, batch, seq, mod[l], 0, batch * seq, lw, None)
        new_k.append(kl.reshape(batch, seq, N_KV, HEAD_DIM))
        new_v.append(vl.reshape(batch, seq, N_KV, HEAD_DIM))
        xs, _, _ = _layer(xs, dec_batch, dec_seq, mod[l], 1, dec_seq, lw, (ck, cv, l, cos_t, sin_t))
    return (xp.reshape(batch, seq, D_MODEL), xs.reshape(dec_batch, dec_seq, D_MODEL),
            jnp.stack(new_k, axis=1), jnp.stack(new_v, axis=1))
```

```python
import functools

import jax
import jax.numpy as jnp
from jax import lax
from jax.experimental import pallas as pl
from jax.experimental.pallas import tpu as pltpu

D_MODEL = 2048
DEPTH = 2
GRID_W = 64
HEAD_DIM = 128
N_HEADS = 8
N_KV = 2
GROUP = N_HEADS // N_KV
ATTN_W = N_HEADS * HEAD_DIM
KV_W = N_KV * HEAD_DIM
SGU_GROUPS = 4
SGU_W = 512
FNET_GROUPS = 4
FNET_W = 512
FNET_GD = 128
PROJ_W = ATTN_W + 2 * KV_W + 2 * SGU_W + FNET_W
BLOCK = 128
WINDOW = 128
CHUNK = 128
ROPE_BASE = 10000.0
N_MOD = 6
PEER_HEADS = 8
PEER_NKEYS = 128
PEER_N = PEER_NKEYS * PEER_NKEYS
PEER_TOPK = 16
PEER_QDIM = 128
PEER_HALF = PEER_QDIM // 2
DN_ALPHA = (2 * DEPTH) ** 0.25
LN_EPS = 1e-5
NEG_INF = -1e30

F32 = jnp.float32
BF16 = jnp.bfloat16
FP8 = jnp.float8_e4m3fn
FP8_TOP = 256.0
LANES = 128
SUBLANES = 8
V7X_VMEM_BYTES = 64 * 2 ** 20
VMEM_LIMIT = V7X_VMEM_BYTES - 8 * 2 ** 20
ROW_TILE = 256
TOPK_TOKEN_TILE = 1024
PEER_TOKEN_TILE = 512
PEER_EXPERT_SUBTILE = 128
PEER_OUT_ROWS = 256
PEER_EXPERT_TILE = SUBLANES * PEER_NKEYS
MOD_COL_TILE = 1024
MOD_ROWS = 16

_NT = (((1,), (1,)), ((), ()))


def _cparams(*sem):
    return pltpu.CompilerParams(dimension_semantics=sem, vmem_limit_bytes=VMEM_LIMIT)


def _row_reduce(op, lane_op, *blocks):
    acc = None
    for x in blocks:
        for i in range(x.shape[1] // LANES):
            chunk = x[:, i * LANES:(i + 1) * LANES]
            acc = chunk if acc is None else op(acc, chunk)
    return lane_op(acc, axis=-1, keepdims=True)


def _ln(x):
    inv_n = 1.0 / x.shape[-1]
    xc = x - _row_reduce(jnp.add, jnp.sum, x) * inv_n
    var = _row_reduce(jnp.add, jnp.sum, xc * xc) * inv_n
    return xc * lax.rsqrt(var + LN_EPS)


def _gelu(x):
    return 0.5 * x * (1.0 + lax.erf(x * (0.5 ** 0.5)))


def _pack_bf16(x):
    return pltpu.bitcast(x.astype(BF16), jnp.uint32)


def _unpack_bf16(words):
    return pltpu.bitcast(words, BF16)


def _fp8_scale(amax):
    return jnp.where(amax > 0.0, amax, FP8_TOP) * (1.0 / FP8_TOP)


def _pack_fp8(x):
    return pltpu.bitcast(x.astype(FP8), jnp.uint32)


def _unpack_fp8(words):
    return pltpu.bitcast(words, FP8)


def _pack_fp8_cols(xt):
    scale = _fp8_scale(jnp.max(jnp.abs(xt), axis=0, keepdims=True))
    return _pack_fp8(xt / scale), scale, jnp.sqrt(jnp.sum(xt * xt, axis=0, keepdims=True))


def _pack_rows_kernel(x_ref, o_ref, s_ref, n_ref, *, transpose):
    x = x_ref[...]
    scale = _fp8_scale(_row_reduce(jnp.maximum, jnp.max, jnp.abs(x)))
    xs = x / scale
    o_ref[...] = _pack_fp8(xs.T if transpose else xs)
    s_ref[...] = jnp.broadcast_to(scale, s_ref.shape)
    n_ref[...] = jnp.broadcast_to(jnp.sqrt(_row_reduce(jnp.add, jnp.sum, x * x)), n_ref.shape)


def _pack_table_fp8(table, layer, transpose):
    _, n, d = table.shape
    rows = PEER_EXPERT_TILE
    if transpose:
        out_spec = pl.BlockSpec((d // 4, rows), lambda e: (0, e))
        out_shape = jax.ShapeDtypeStruct((d // 4, n), jnp.uint32)
    else:
        out_spec = pl.BlockSpec((rows // 4, d), lambda e: (e, 0))
        out_shape = jax.ShapeDtypeStruct((n // 4, d), jnp.uint32)
    per_row = pl.BlockSpec((rows, LANES), lambda e: (e, 0))
    per_row_shape = jax.ShapeDtypeStruct((n, LANES), F32)
    return pl.pallas_call(
        functools.partial(_pack_rows_kernel, transpose=transpose),
        grid=(n // rows,),
        in_specs=[pl.BlockSpec((None, rows, d), lambda e: (layer, e, 0))],
        out_specs=[out_spec, per_row, per_row],
        out_shape=[out_shape, per_row_shape, per_row_shape],
        compiler_params=_cparams("arbitrary"),
        name="pack_table_fp8_t" if transpose else "pack_table_fp8",
    )(table)


def _mod_kernel(c_ref, w_ref, b_ref, o_ref):
    c = c_ref[...]
    a = (c / (1.0 + jnp.exp(-c))).astype(BF16)
    o_ref[...] = jnp.dot(a, w_ref[...].astype(BF16), preferred_element_type=F32) + b_ref[...]


def _modulation(cvec, w_mod, b_mod):
    n = N_MOD * D_MODEL
    out = pl.pallas_call(
        _mod_kernel,
        grid=(DEPTH, n // MOD_COL_TILE),
        in_specs=[
            pl.BlockSpec((MOD_ROWS, D_MODEL), lambda l, j: (0, 0)),
            pl.BlockSpec((None, D_MODEL, MOD_COL_TILE), lambda l, j: (l, 0, j)),
            pl.BlockSpec((None, 1, MOD_COL_TILE), lambda l, j: (l, 0, j)),
        ],
        out_specs=pl.BlockSpec((None, MOD_ROWS, MOD_COL_TILE), lambda l, j: (l, 0, j)),
        out_shape=jax.ShapeDtypeStruct((DEPTH, MOD_ROWS, n), F32),
        compiler_params=_cparams("arbitrary", "arbitrary"),
        name="modulation",
    )(cvec, w_mod, b_mod.reshape(DEPTH, 1, n))
    return out.reshape(DEPTH, MOD_ROWS, N_MOD, D_MODEL)


_PROJ_SLICES = ((0, ATTN_W), (ATTN_W, KV_W), (ATTN_W + KV_W, KV_W),
                (ATTN_W + 2 * KV_W, SGU_W), (ATTN_W + 2 * KV_W + SGU_W, SGU_W),
                (ATTN_W + 2 * KV_W + 2 * SGU_W, FNET_W))


def _proj_kernel(x_ref, mod_ref, w_ref, g_ref, ws_ref, bias_ref, q_ref, k_ref, v_ref, sg_ref, f_ref):
    h = _ln(x_ref[...]) * (1.0 + mod_ref[1:2, :]) + mod_ref[0:1, :]
    h = h.astype(BF16)
    proj = [jnp.dot(h, w_ref[:, start:start + width], preferred_element_type=F32)
            for start, width in _PROJ_SLICES]
    q_ref[...], k_ref[...], v_ref[...], f_ref[...] = proj[0], proj[1], proj[2], proj[5]
    _spatial_gate(proj[3], proj[4], g_ref, ws_ref, bias_ref, sg_ref)


def _in_projection(x, mod, mod_base, rows_per_mod, w_in, sgu_g, sgu_w, sgu_bias):
    t = x.shape[0]
    tiles_per_mod = rows_per_mod // ROW_TILE
    row = lambda w: pl.BlockSpec((ROW_TILE, w), lambda i: (i, 0))
    widths = (ATTN_W, KV_W, KV_W, SGU_W, FNET_W)
    return pl.pallas_call(
        _proj_kernel,
        grid=(t // ROW_TILE,),
        in_specs=[
            row(D_MODEL),
            pl.BlockSpec((None, N_MOD, D_MODEL), lambda i: (mod_base + i // tiles_per_mod, 0, 0)),
            pl.BlockSpec((D_MODEL, PROJ_W), lambda i: (0, 0)),
            pl.BlockSpec((1, SGU_W), lambda i: (0, 0)),
            pl.BlockSpec((SGU_GROUPS, CHUNK, CHUNK), lambda i: (0, 0, 0)),
            pl.BlockSpec((CHUNK, SGU_W), lambda i: (0, 0)),
        ],
        out_specs=[row(w) for w in widths],
        out_shape=[jax.ShapeDtypeStruct((t, w), F32) for w in widths],
        compiler_params=_cparams("arbitrary"),
        name="in_projection",
    )(x, mod, w_in, sgu_g, sgu_w, sgu_bias)


def _head_cols(g):
    return slice(g * HEAD_DIM, (g + 1) * HEAD_DIM)


def _stacked_sinks(sink_ref, rows):
    return jnp.concatenate([jnp.broadcast_to(sink_ref[g:g + 1, 0:1], (rows, 1)) for g in range(GROUP)],
                           axis=0)


def _ctx_attn_kernel(q_ref, k_ref, v_ref, sink_ref, o_ref):
    seq = q_ref.shape[0]
    k = k_ref[...].astype(BF16)
    v = v_ref[...].astype(BF16)
    q = jnp.concatenate([q_ref[:, _head_cols(g)] for g in range(GROUP)], axis=0).astype(BF16)
    s = lax.dot_general(q, k, _NT, preferred_element_type=F32) * (HEAD_DIM ** -0.5)
    sk = _stacked_sinks(sink_ref, seq)
    m = jnp.maximum(_row_reduce(jnp.maximum, jnp.max, s), sk)
    p = jnp.exp(s - m)
    den = _row_reduce(jnp.add, jnp.sum, p) + jnp.exp(sk - m)
    o = jnp.dot(p.astype(BF16), v, preferred_element_type=F32) / den
    for g in range(GROUP):
        o_ref[:, _head_cols(g)] = o[g * seq:(g + 1) * seq]


def _context_attention(q, k, v, sink_b, batch, seq):
    t = q.shape[0]
    qspec = pl.BlockSpec((seq, GROUP * HEAD_DIM), lambda b, h: (b, h))
    kvspec = pl.BlockSpec((seq, HEAD_DIM), lambda b, h: (b, h))
    return pl.pallas_call(
        _ctx_attn_kernel,
        grid=(batch, N_KV),
        in_specs=[qspec, kvspec, kvspec, pl.BlockSpec((None, SUBLANES, LANES), lambda b, h: (h, 0, 0))],
        out_specs=qspec,
        out_shape=jax.ShapeDtypeStruct((t, ATTN_W), F32),
        compiler_params=_cparams("arbitrary", "arbitrary"),
        name="context_attention",
    )(q, k, v, sink_b)


def _lat_attn_kernel(q_ref, k_ref, v_ref, ck_ref, cv_ref, cos_ref, sin_ref, sink_ref, o_ref,
                     kp_ref, vp_ref, *, seq):
    scale = HEAD_DIM ** -0.5
    lane = lax.broadcasted_iota(jnp.int32, (1, HEAD_DIM), 1)
    first = (lane % 64) < 32

    def rope(x, cos, sin):
        sw = jnp.where(first, pltpu.roll(x, 96, 1), pltpu.roll(x, 32, 1))
        return x * cos + sw * sin

    pad = jnp.zeros((BLOCK, HEAD_DIM), BF16)
    kp_ref[0:BLOCK, :] = pad
    kp_ref[seq + BLOCK:seq + 2 * BLOCK, :] = pad
    vp_ref[0:BLOCK, :] = pad
    vp_ref[seq + BLOCK:seq + 2 * BLOCK, :] = pad
    kp_ref[BLOCK:seq + BLOCK, :] = rope(k_ref[...], cos_ref[...], sin_ref[...]).astype(BF16)
    vp_ref[BLOCK:seq + BLOCK, :] = v_ref[...].astype(BF16)
    ck = ck_ref[...].astype(BF16)
    cv = cv_ref[...].astype(BF16)
    sk = _stacked_sinks(sink_ref, BLOCK)

    def body(qb, carry):
        r0 = pl.multiple_of(qb * BLOCK, BLOCK)
        kb = kp_ref[pl.ds(r0, 3 * BLOCK), :]
        vb = vp_ref[pl.ds(r0, 3 * BLOCK), :]
        cosq = cos_ref[pl.ds(r0, BLOCK), :]
        sinq = sin_ref[pl.ds(r0, BLOCK), :]
        r = lax.broadcasted_iota(jnp.int32, (GROUP * BLOCK, 3 * BLOCK), 0) & (BLOCK - 1)
        j = lax.broadcasted_iota(jnp.int32, (GROUP * BLOCK, 3 * BLOCK), 1)
        kpos = j + (r0 - BLOCK)
        valid = (j >= r) & (j <= r + 2 * WINDOW) & (kpos >= 0) & (kpos < seq)
        q = jnp.concatenate([rope(q_ref[pl.ds(r0, BLOCK), _head_cols(g)], cosq, sinq)
                             for g in range(GROUP)], axis=0).astype(BF16)
        sl = lax.dot_general(q, kb, _NT, preferred_element_type=F32) * scale
        sl = jnp.where(valid, sl, NEG_INF)
        sc = lax.dot_general(q, ck, _NT, preferred_element_type=F32) * scale
        m = jnp.maximum(_row_reduce(jnp.maximum, jnp.max, sl, sc), sk)
        p_l = jnp.exp(sl - m)
        p_c = jnp.exp(sc - m)
        den = _row_reduce(jnp.add, jnp.sum, p_l, p_c) + jnp.exp(sk - m)
        o = (jnp.dot(p_l.astype(BF16), vb, preferred_element_type=F32)
             + jnp.dot(p_c.astype(BF16), cv, preferred_element_type=F32)) / den
        for g in range(GROUP):
            o_ref[pl.ds(r0, BLOCK), _head_cols(g)] = o[g * BLOCK:(g + 1) * BLOCK]
        return carry

    lax.fori_loop(0, seq // BLOCK, body, 0, unroll=4)


def _latent_attention(q, k, v, cache_k, cache_v, layer, cos_t, sin_t, sink_b, batch, seq):
    t = q.shape[0]
    past = cache_k.shape[2]
    qspec = pl.BlockSpec((seq, GROUP * HEAD_DIM), lambda b, h: (b, h))
    kvspec = pl.BlockSpec((seq, HEAD_DIM), lambda b, h: (b, h))
    cspec = pl.BlockSpec((None, None, past, HEAD_DIM), lambda b, h: (b, layer, 0, h))
    tspec = pl.BlockSpec((seq, HEAD_DIM), lambda b, h: (0, 0))
    return pl.pallas_call(
        functools.partial(_lat_attn_kernel, seq=seq),
        grid=(batch, N_KV),
        in_specs=[qspec, kvspec, kvspec, cspec, cspec, tspec, tspec,
                  pl.BlockSpec((None, SUBLANES, LANES), lambda b, h: (h, 0, 0))],
        out_specs=qspec,
        out_shape=jax.ShapeDtypeStruct((t, ATTN_W), F32),
        scratch_shapes=[pltpu.VMEM((seq + 2 * BLOCK, HEAD_DIM), BF16),
                        pltpu.VMEM((seq + 2 * BLOCK, HEAD_DIM), BF16)],
        compiler_params=_cparams("arbitrary", "arbitrary"),
        name="latent_attention",
    )(q, k, v, cache_k, cache_v, cos_t, sin_t, sink_b)


def _rope_tables(seq):
    pos = jnp.arange(seq)
    r = (pos // GRID_W).astype(F32)
    col = (pos % GRID_W).astype(F32)
    n = HEAD_DIM // 4
    inv = ROPE_BASE ** (-jnp.arange(n, dtype=F32) / n)
    ang_r = r[:, None] * inv
    ang_c = col[:, None] * inv
    cos_t = jnp.concatenate([jnp.cos(ang_r), jnp.cos(ang_r), jnp.cos(ang_c), jnp.cos(ang_c)], axis=-1)
    sin_t = jnp.concatenate([-jnp.sin(ang_r), jnp.sin(ang_r), -jnp.sin(ang_c), jnp.sin(ang_c)], axis=-1)
    return cos_t, sin_t


def _spatial_gate(ug, vg, g_ref, ws_ref, bias_ref, o_ref):
    for c in range(ROW_TILE // CHUNK):
        rows = slice(c * CHUNK, (c + 1) * CHUNK)
        for g in range(SGU_GROUPS):
            cols = slice(g * LANES, (g + 1) * LANES)
            u = _gelu(ug[rows, cols])
            vn = _ln(_gelu(vg[rows, cols])) * g_ref[0:1, cols]
            mixed = jnp.dot(ws_ref[g], vn.astype(BF16), preferred_element_type=F32) + bias_ref[:, cols]
            o_ref[rows, cols] = u * mixed


def _fnet_kernel(f_ref, cc_ref, cs_ref, o_ref, z_ref, *, seq):
    @pl.when(pl.program_id(1) == 0)
    def _():
        cc = cc_ref[...]
        for g in range(FNET_GROUPS):
            cols = slice(g * FNET_GD, (g + 1) * FNET_GD)
            zz = jnp.dot(f_ref[:, cols].astype(BF16), cc, preferred_element_type=F32)
            z_ref[0:seq, cols] = zz[:, :FNET_GD].astype(BF16)
            z_ref[seq:2 * seq, cols] = zz[:, FNET_GD:].astype(BF16)

    y = jnp.dot(cs_ref[...], z_ref[...], preferred_element_type=F32)
    o_ref[...] = y * ((seq * FNET_GD) ** -0.5)


def _dft_tables(n):
    k = jnp.arange(n, dtype=jnp.int32)
    ang = ((k[:, None] * k[None, :]) % n).astype(F32) * (2.0 * jnp.pi / n)
    return jnp.cos(ang), jnp.sin(ang)


def _fourier_mix(f, batch, seq):
    t = f.shape[0]
    tr = min(seq, 512)
    c_c, s_c = _dft_tables(FNET_GD)
    c_s, s_s = _dft_tables(seq)
    cc = jnp.concatenate([c_c, s_c], axis=1).astype(BF16)
    cs = jnp.concatenate([c_s, -s_s], axis=1).astype(BF16)
    nj = seq // tr
    return pl.pallas_call(
        functools.partial(_fnet_kernel, seq=seq),
        grid=(batch, nj),
        in_specs=[pl.BlockSpec((seq, FNET_W), lambda b, j: (b, 0)),
                  pl.BlockSpec((FNET_GD, 2 * FNET_GD), lambda b, j: (0, 0)),
                  pl.BlockSpec((tr, 2 * seq), lambda b, j: (j, 0))],
        out_specs=pl.BlockSpec((tr, FNET_W), lambda b, j: (b * nj + j, 0)),
        out_shape=jax.ShapeDtypeStruct((t, FNET_W), F32),
        scratch_shapes=[pltpu.VMEM((2 * seq, FNET_W), BF16)],
        compiler_params=_cparams("arbitrary", "arbitrary"),
        name="fourier_mix",
    )(f, cc, cs)


def _mix_kernel(x_ref, a_ref, s_ref, f_ref, wo_ref, mod_ref, g1_ref, b1_ref, wq_ref,
                x1_ref, h2_ref, hs_ref, hn_ref, qp_ref):
    mix = jnp.dot(a_ref[...].astype(BF16), wo_ref[0:ATTN_W, :], preferred_element_type=F32)
    mix += jnp.dot(s_ref[...].astype(BF16), wo_ref[ATTN_W:ATTN_W + SGU_W, :], preferred_element_type=F32)
    mix += jnp.dot(f_ref[...].astype(BF16), wo_ref[ATTN_W + SGU_W:, :], preferred_element_type=F32)
    x1 = _ln(DN_ALPHA * x_ref[...] + mod_ref[2:3, :] * mix) * g1_ref[...] + b1_ref[...]
    x1_ref[...] = x1
    h2 = _ln(x1) * (1.0 + mod_ref[4:5, :]) + mod_ref[3:4, :]
    h2_ref[...], hs_ref[...], hn_ref[...] = _pack_fp8_cols(h2.T)
    q = jnp.dot(h2.astype(BF16), wq_ref[...], preferred_element_type=F32)
    for h in range(PEER_HEADS):
        qp_ref[h] = q[:, h * PEER_QDIM:(h + 1) * PEER_QDIM].astype(BF16)


def _out_projection(x, attn, sg, fn, w_out, mod, mod_base, rows_per_mod, ln_g, ln_b, wq):
    t = x.shape[0]
    tiles_per_mod = rows_per_mod // ROW_TILE
    row = lambda w: pl.BlockSpec((ROW_TILE, w), lambda i: (i, 0))
    full = lambda a, b: pl.BlockSpec((a, b), lambda i: (0, 0))
    return pl.pallas_call(
        _mix_kernel,
        grid=(t // ROW_TILE,),
        in_specs=[row(D_MODEL), row(ATTN_W), row(SGU_W), row(FNET_W),
                  full(D_MODEL, D_MODEL),
                  pl.BlockSpec((None, N_MOD, D_MODEL), lambda i: (mod_base + i // tiles_per_mod, 0, 0)),
                  full(1, D_MODEL), full(1, D_MODEL),
                  full(D_MODEL, PEER_HEADS * PEER_QDIM)],
        out_specs=[row(D_MODEL), pl.BlockSpec((D_MODEL // 4, ROW_TILE), lambda i: (0, i)),
                   pl.BlockSpec((1, ROW_TILE), lambda i: (0, i)),
                   pl.BlockSpec((1, ROW_TILE), lambda i: (0, i)),
                   pl.BlockSpec((PEER_HEADS, ROW_TILE, PEER_QDIM), lambda i: (0, i, 0))],
        out_shape=[jax.ShapeDtypeStruct((t, D_MODEL), F32),
                   jax.ShapeDtypeStruct((D_MODEL // 4, t), jnp.uint32),
                   jax.ShapeDtypeStruct((1, t), F32),
                   jax.ShapeDtypeStruct((1, t), F32),
                   jax.ShapeDtypeStruct((PEER_HEADS, t, PEER_QDIM), BF16)],
        compiler_params=_cparams("arbitrary"),
        name="out_projection",
    )(x, attn, sg, fn, w_out, mod, ln_g, ln_b, wq)


def _pair_candidates(v1, v2):
    half = PEER_TOPK // 2
    pieces = [v1[0:1] + v2] + [v1[r1:r1 + 1] + v2[0:half] for r1 in range(1, PEER_TOPK)]
    pos_pieces = [lax.broadcasted_iota(jnp.int32, (PEER_TOPK, LANES), 0)]
    for r1 in range(1, PEER_TOPK):
        pos_pieces.append(lax.broadcasted_iota(jnp.int32, (half, LANES), 0) + r1 * PEER_TOPK)
    spans = [(0, PEER_TOPK)] + [(PEER_TOPK + (r1 - 1) * half, half) for r1 in range(1, PEER_TOPK)]
    return jnp.concatenate(pieces, axis=0), jnp.concatenate(pos_pieces, axis=0).astype(F32), spans


def _descending_maxima(slabs):
    w = slabs[0].shape[1]
    slot = lax.broadcasted_iota(jnp.int32, (PEER_TOPK, w), 0)

    def body(k, carry):
        out = []
        for s, (prev, vals) in zip(slabs, carry):
            m = jnp.max(jnp.where(s < prev, s, -jnp.inf), axis=0, keepdims=True)
            out.append((m, jnp.where(slot == k, m, vals)))
        return tuple(out)

    init = tuple((jnp.full((1, w), jnp.inf, F32), jnp.zeros((PEER_TOPK, w), F32)) for _ in slabs)
    return [vals for _, vals in lax.fori_loop(0, PEER_TOPK, body, init)]


def _count_rows(mask):
    return jnp.sum(jnp.where(mask, 1.0, 0.0), axis=0, keepdims=True)


def _topk_tables_distinct(s1, s2):
    nc = len(s1)
    k = float(PEER_TOPK)
    vals = _descending_maxima(s1 + s2)
    v1, v2 = vals[:nc], vals[nc:]
    cands = [_pair_candidates(a, b) for a, b in zip(v1, v2)]
    best = _descending_maxima([cand for cand, _, _ in cands])
    tables, bad = [], jnp.zeros((1, LANES), F32)
    for c in range(nc):
        cand, _, spans = cands[c]
        taken = jnp.where(cand >= best[c][PEER_TOPK - 1:PEER_TOPK], 1.0, 0.0)
        cnt = [jnp.sum(taken[lo:lo + n], axis=0, keepdims=True) for lo, n in spans]
        z = jnp.sum(jnp.exp(best[c] - best[c][0:1]), axis=0, keepdims=True)
        rank2 = jnp.zeros((PEER_NKEYS, LANES), F32)
        n1 = jnp.zeros((PEER_NKEYS, LANES), F32)
        for r in range(PEER_TOPK):
            rank2 = rank2 + jnp.where(v2[c][r:r + 1] > s2[c], 1.0, 0.0)
            n1 = jnp.where(s1[c] == v1[c][r:r + 1], cnt[r], n1)
        tables.append((rank2, jnp.exp(s2[c] - v2[c][0:1]), n1, jnp.exp(s1[c] - v1[c][0:1]) / z))
        for s, v in ((s1[c], v1[c]), (s2[c], v2[c])):
            bad = bad + jnp.abs(_count_rows(s >= v[PEER_TOPK - 1:PEER_TOPK]) - k)
        bad = bad + jnp.abs(sum(cnt) - k)
    return tables, bad


def _top16_rows(scores):
    n, w = scores[0].shape
    row = lax.broadcasted_iota(jnp.int32, (n, w), 0).astype(F32)
    slot = lax.broadcasted_iota(jnp.int32, (PEER_TOPK, w), 0)

    def body(k, carry):
        kf = lax.convert_element_type(k, F32)
        out = []
        for cur, rank, vals in carry:
            m = jnp.max(cur, axis=0, keepdims=True)
            hit = row == jnp.min(jnp.where(cur == m, row, float(n)), axis=0, keepdims=True)
            out.append((jnp.where(hit, -jnp.inf, cur), jnp.where(hit, kf, rank),
                        jnp.where(slot == k, m, vals)))
        return tuple(out)

    init = tuple((s, jnp.full((n, w), float(PEER_TOPK), F32), jnp.zeros((PEER_TOPK, w), F32))
                 for s in scores)
    res = lax.fori_loop(0, PEER_TOPK, body, init)
    return [(rank, vals) for _, rank, vals in res]


def _topk_tables_exact(s1, s2):
    nc = len(s1)
    tops = _top16_rows(s1 + s2)
    slot = lax.broadcasted_iota(jnp.int32, (PEER_TOPK, LANES), 0).astype(F32)
    cands = [_pair_candidates(tops[c][1], tops[nc + c][1]) for c in range(nc)]

    def body(k, carry):
        out = []
        for (cand, cnt, z), (cand0, pos, _) in zip(carry, cands):
            m = jnp.max(cand, axis=0, keepdims=True)
            p = jnp.min(jnp.where(cand == m, pos, 4096.0), axis=0, keepdims=True)
            out.append((jnp.where(pos == p, -jnp.inf, cand),
                        cnt + jnp.where(slot == jnp.floor(p * (1.0 / PEER_TOPK)), 1.0, 0.0),
                        z + jnp.exp(m - cand0[0:1])))
        return tuple(out)

    init = tuple((cand, jnp.zeros((PEER_TOPK, LANES), F32), jnp.zeros((1, LANES), F32))
                 for cand, _, _ in cands)
    res = lax.fori_loop(0, PEER_TOPK, body, init)
    tables = []
    for c in range(nc):
        (rank1, v1), (rank2, v2) = tops[c], tops[nc + c]
        _, cnt, z = res[c]
        n1 = jnp.zeros((PEER_NKEYS, LANES), F32)
        for r1 in range(PEER_TOPK):
            n1 = jnp.where(rank1 == float(r1), cnt[r1:r1 + 1], n1)
        tables.append((rank2, jnp.exp(s2[c] - v2[0:1]), n1, jnp.exp(s1[c] - v1[0:1]) / z))
    return tables


def _topk_kernel(q_ref, km_ref, r2_ref, e2_ref, n1_ref, e1_ref, *, tm):
    km = km_ref[...]
    toks = [slice(c * LANES, (c + 1) * LANES) for c in range(tm // LANES)]
    sts = [lax.dot_general(km, q_ref[tok, :], _NT, preferred_element_type=F32) for tok in toks]
    s1 = [st[0:PEER_NKEYS] for st in sts]
    s2 = [st[PEER_NKEYS:2 * PEER_NKEYS] for st in sts]

    def store(tables):
        for tok, (rank2, e2, n1, e1) in zip(toks, tables):
            r2_ref[:, tok] = _pack_bf16(rank2)
            e2_ref[:, tok] = _pack_bf16(e2)
            n1_ref[:, tok] = n1
            e1_ref[:, tok] = e1

    tables, bad = _topk_tables_distinct(s1, s2)
    store(tables)

    @pl.when(jnp.max(bad) > 0.0)
    def _():
        store(_topk_tables_exact(s1, s2))


def _peer_topk(qp, kmat):
    t = qp.shape[1]
    tm = min(TOPK_TOKEN_TILE, t)
    slab_spec = pl.BlockSpec((PEER_NKEYS // 2, tm), lambda i, h: (h, i))
    slab_shape = jax.ShapeDtypeStruct((PEER_HEADS * PEER_NKEYS // 2, t), jnp.uint32)
    row_spec = pl.BlockSpec((PEER_NKEYS, tm), lambda i, h: (h, i))
    row_shape = jax.ShapeDtypeStruct((PEER_HEADS * PEER_NKEYS, t), F32)
    return pl.pallas_call(
        functools.partial(_topk_kernel, tm=tm),
        grid=(t // tm, PEER_HEADS),
        in_specs=[pl.BlockSpec((None, tm, PEER_QDIM), lambda i, h: (h, i, 0)),
                  pl.BlockSpec((None, 2 * PEER_NKEYS, PEER_QDIM), lambda i, h: (h, 0, 0))],
        out_specs=[slab_spec, slab_spec, row_spec, row_spec],
        out_shape=[slab_shape, slab_shape, row_shape, row_shape],
        compiler_params=_cparams("arbitrary", "arbitrary"),
        name="peer_topk",
    )(qp, kmat)


def _peer_kernel(h_ref, hs_ref, iw_ref, u_ref, us_ref, vt_ref, vs_ref, r2_ref, e2_ref, n1_ref, e1_ref,
                 o_ref, g_ref, w_ref, *, tm, n_steps):
    e = pl.program_id(1)

    @pl.when(e == 0)
    def _():
        o_ref[...] = jnp.zeros_like(o_ref)
        _peer_gated_activations(e, 0, h_ref, hs_ref, iw_ref, u_ref, us_ref, vs_ref, r2_ref, e2_ref,
                                n1_ref, e1_ref, g_ref, w_ref, tm)

    @pl.when((e > 0) & (e < n_steps))
    def _():
        chunks = _peer_accumulate_chunks((e - 1) % 2, vt_ref, w_ref, o_ref)
        _peer_gated_activations(e, e % 2, h_ref, hs_ref, iw_ref, u_ref, us_ref, vs_ref, r2_ref, e2_ref,
                                n1_ref, e1_ref, g_ref, w_ref, tm, between=chunks)

    @pl.when(e == n_steps)
    def _():
        _peer_accumulate((n_steps - 1) % 2, vt_ref, w_ref, o_ref)


def _peer_accumulate_chunks(slot, vt_ref, w_ref, o_ref):
    def chunk(r0):
        rows = slice(r0, r0 + PEER_OUT_ROWS)
        o_ref[rows, :] += jnp.dot(_unpack_fp8(vt_ref[r0 // 4:(r0 + PEER_OUT_ROWS) // 4, :]),
                                  _unpack_fp8(w_ref[slot]), preferred_element_type=F32)
    return [functools.partial(chunk, r0) for r0 in range(0, D_MODEL, PEER_OUT_ROWS)]


def _peer_accumulate(slot, vt_ref, w_ref, o_ref):
    for chunk in _peer_accumulate_chunks(slot, vt_ref, w_ref, o_ref):
        chunk()


def _peer_gated_activations(e, slot, h_ref, hs_ref, iw_ref, u_ref, us_ref, vs_ref, r2_ref, e2_ref,
                            n1_ref, e1_ref, g_ref, w_ref, tm, between=()):
    between = list(between)
    assert len(between) in (0, SUBLANES)
    words = PEER_NKEYS // 2
    quads = PEER_NKEYS // 4
    pack = 2 * SUBLANES
    groups = PEER_NKEYS // pack

    for bi in range(SUBLANES):
        experts = slice(bi * PEER_NKEYS, (bi + 1) * PEER_NKEYS)
        for c in range(tm // LANES):
            tok = slice(c * LANES, (c + 1) * LANES)
            gate = jnp.zeros((groups, pack, LANES), BF16)
            for h in range(PEER_HEADS):
                keys = slice(h * words, (h + 1) * words)
                grp = pl.ds(pl.multiple_of(h * PEER_NKEYS + e * SUBLANES, SUBLANES), SUBLANES)
                n1 = jnp.broadcast_to(n1_ref[grp, tok][bi:bi + 1, :], (pack, LANES)).astype(BF16)
                e1 = jnp.broadcast_to(e1_ref[grp, tok][bi:bi + 1, :], (pack, LANES)).astype(BF16)
                r2 = _unpack_bf16(r2_ref[keys, tok]).reshape(groups, pack, LANES)
                e2 = _unpack_bf16(e2_ref[keys, tok]).reshape(groups, pack, LANES)
                gate += jnp.where(r2 < n1[None], e2, 0.0) * e1[None]
            rescale = (vs_ref[experts, :] * iw_ref[:, tok]).astype(BF16)
            g_ref[bi * words:(bi + 1) * words, tok] = pltpu.bitcast(
                gate.reshape(PEER_NKEYS, LANES) * rescale, jnp.uint32)
        if between:
            between[bi]()

    ht = _unpack_fp8(h_ref[...])
    per_sub = PEER_EXPERT_SUBTILE // PEER_NKEYS
    for sb in range(PEER_EXPERT_TILE // PEER_EXPERT_SUBTILE):
        urows = slice(sb * PEER_EXPERT_SUBTILE // 4, (sb + 1) * PEER_EXPERT_SUBTILE // 4)
        at = jnp.dot(_unpack_fp8(u_ref[urows, :]), ht, preferred_element_type=F32)
        for bs in range(per_sub):
            bi = sb * per_sub + bs
            half_us = 0.5 * us_ref[bi * PEER_NKEYS:(bi + 1) * PEER_NKEYS, :]
            for c in range(tm // LANES):
                tok = slice(c * LANES, (c + 1) * LANES)
                y = at[bs * PEER_NKEYS:(bs + 1) * PEER_NKEYS, tok] * half_us * hs_ref[:, tok]
                act = (y * (1.0 + lax.erf(y * (2.0 ** 0.5)))).astype(BF16)
                gate = _unpack_bf16(g_ref[bi * words:(bi + 1) * words, tok])
                w_ref[slot, bi * quads:(bi + 1) * quads, tok] = pltpu.bitcast((act * gate).astype(FP8),
                                                                                jnp.uint32)


def _peer_weight_scale(h2_norm, u_norm, v_scale):
    bound = PEER_HEADS * jnp.max(u_norm[:, 0] * v_scale[:, 0]) * h2_norm
    return jnp.where(bound > 0.0, bound, FP8_TOP) * (1.0 / FP8_TOP)


def _peer_apply(h2t, h2_scale, w_scale, u, u_scale, vt, v_scale, tables):
    t = h2t.shape[1]
    tm = min(PEER_TOKEN_TILE, t)
    eb = PEER_EXPERT_TILE
    slab = pl.BlockSpec((PEER_HEADS * PEER_NKEYS // 2, tm), lambda i, e: (0, i))
    rowt = pl.BlockSpec((PEER_HEADS * PEER_NKEYS, tm), lambda i, e: (0, i))
    n_steps = PEER_N // eb
    per_token = pl.BlockSpec((1, tm), lambda i, e: (0, i))
    this_block = lambda e: jnp.minimum(e, n_steps - 1)
    prev_block = lambda e: jnp.maximum(e - 1, 0)
    per_expert = pl.BlockSpec((eb, LANES), lambda i, e: (this_block(e), 0))
    return pl.pallas_call(
        functools.partial(_peer_kernel, tm=tm, n_steps=n_steps),
        grid=(t // tm, n_steps + 1),
        in_specs=[pl.BlockSpec((D_MODEL // 4, tm), lambda i, e: (0, i)), per_token, per_token,
                  pl.BlockSpec((eb // 4, D_MODEL), lambda i, e: (this_block(e), 0)), per_expert,
                  pl.BlockSpec((D_MODEL // 4, eb), lambda i, e: (0, prev_block(e))), per_expert,
                  slab, slab, rowt, rowt],
        out_specs=pl.BlockSpec((D_MODEL, tm), lambda i, e: (0, i)),
        out_shape=jax.ShapeDtypeStruct((D_MODEL, t), F32),
        scratch_shapes=[pltpu.VMEM((eb // 2, tm), jnp.uint32),
                        pltpu.VMEM((2, eb // 4, tm), jnp.uint32)],
        compiler_params=_cparams("arbitrary", "arbitrary"),
        name="peer_apply",
    )(h2t, h2_scale, 1.0 / w_scale, u, u_scale, vt, v_scale, *tables)


def _final_kernel(x_ref, pt_ref, ws_ref, mod_ref, g_ref, b_ref, o_ref):
    peer = (pt_ref[...] * ws_ref[...]).T
    y = DN_ALPHA * x_ref[...] + mod_ref[5:6, :] * peer
    o_ref[...] = _ln(y) * g_ref[...] + b_ref[...]


def _finalize(x1, peer_t, w_scale, mod, mod_base, rows_per_mod, ln_g, ln_b):
    t = x1.shape[0]
    tiles_per_mod = rows_per_mod // ROW_TILE
    row = pl.BlockSpec((ROW_TILE, D_MODEL), lambda i: (i, 0))
    vec = pl.BlockSpec((1, D_MODEL), lambda i: (0, 0))
    return pl.pallas_call(
        _final_kernel,
        grid=(t // ROW_TILE,),
        in_specs=[row,
                  pl.BlockSpec((D_MODEL, ROW_TILE), lambda i: (0, i)),
                  pl.BlockSpec((1, ROW_TILE), lambda i: (0, i)),
                  pl.BlockSpec((None, N_MOD, D_MODEL), lambda i: (mod_base + i // tiles_per_mod, 0, 0)),
                  vec, vec],
        out_specs=row,
        out_shape=jax.ShapeDtypeStruct((t, D_MODEL), F32),
        compiler_params=_cparams("arbitrary"),
        name="finalize",
    )(x1, peer_t, w_scale, mod, ln_g, ln_b)


def _prep_layer(l, w_in, attn_sink, sgu_g, sgu_w, sgu_b, w_out, ln1_g, ln1_b,
                peer_wq, peer_keys, peer_u, peer_v, ln2_g, ln2_b):
    keys = peer_keys[l]
    zeros = jnp.zeros_like(keys[:, 0])
    kmat = jnp.concatenate([jnp.concatenate([keys[:, 0], zeros], axis=-1),
                            jnp.concatenate([zeros, keys[:, 1]], axis=-1)], axis=1)
    return dict(
        w_in=w_in[l].astype(BF16),
        sink_b=jnp.broadcast_to(
            jnp.pad(attn_sink[l].reshape(N_KV, GROUP), ((0, 0), (0, SUBLANES - GROUP)))[:, :, None],
            (N_KV, SUBLANES, LANES)),
        sgu_g=sgu_g[l].reshape(1, SGU_W),
        sgu_w=sgu_w[l].astype(BF16),
        sgu_bias=jnp.repeat(sgu_b[l].T, LANES, axis=1),
        w_out=w_out[l].astype(BF16),
        ln1_g=ln1_g[l].reshape(1, D_MODEL), ln1_b=ln1_b[l].reshape(1, D_MODEL),
        wq=peer_wq[l].astype(BF16),
        kmat=kmat.astype(BF16),
        u=_pack_table_fp8(peer_u, l, transpose=False),
        vt=_pack_table_fp8(peer_v, l, transpose=True),
        ln2_g=ln2_g[l].reshape(1, D_MODEL), ln2_b=ln2_b[l].reshape(1, D_MODEL),
    )


def _layer(x, batch, seq, mod, mod_base, rows_per_mod, lw, ctx):
    q, k, v, sg, f = _in_projection(x, mod, mod_base, rows_per_mod, lw["w_in"],
                                    lw["sgu_g"], lw["sgu_w"], lw["sgu_bias"])
    if ctx is None:
        attn = _context_attention(q, k, v, lw["sink_b"], batch, seq)
    else:
        cache_k, cache_v, layer, cos_t, sin_t = ctx
        attn = _latent_attention(q, k, v, cache_k, cache_v, layer, cos_t, sin_t, lw["sink_b"], batch, seq)
    fn = _fourier_mix(f, batch, seq)
    x1, h2t, h2_scale, h2_norm, qp = _out_projection(x, attn, sg, fn, lw["w_out"], mod, mod_base,
                                                     rows_per_mod, lw["ln1_g"], lw["ln1_b"], lw["wq"])
    tables = _peer_topk(qp, lw["kmat"])
    u, u_scale, u_norm = lw["u"]
    vt, v_scale, _ = lw["vt"]
    w_scale = _peer_weight_scale(h2_norm, u_norm, v_scale)
    peer_t = _peer_apply(h2t, h2_scale, w_scale, u, u_scale, vt, v_scale, tables)
    x2 = _finalize(x1, peer_t, w_scale, mod, mod_base, rows_per_mod, lw["ln2_g"], lw["ln2_b"])
    return x2, k, v


def kernel(x_prompt, x_sample, cache_k, cache_v, c, c_ctx, w_mod, b_mod, w_in, attn_sink, sgu_g, sgu_w, sgu_b, w_out, ln1_g, ln1_b, peer_wq, peer_keys, peer_u, peer_v, ln2_g, ln2_b):
    batch, seq, _ = x_prompt.shape
    dec_batch, dec_seq, _ = x_sample.shape
    past = cache_k.shape[2]
    assert 1 + dec_batch <= MOD_ROWS

    cvec = jnp.concatenate([c_ctx[None, :], c, jnp.zeros((MOD_ROWS - 1 - dec_batch, D_MODEL), F32)], axis=0)
    mod = _modulation(cvec, w_mod, b_mod)
    cos_t, sin_t = _rope_tables(dec_seq)
    ck = cache_k.reshape(dec_batch, DEPTH, past, KV_W)
    cv = cache_v.reshape(dec_batch, DEPTH, past, KV_W)

    xp = x_prompt.reshape(batch * seq, D_MODEL)
    xs = x_sample.reshape(dec_batch * dec_seq, D_MODEL)
    new_k, new_v = [], []
    for l in range(DEPTH):
        lw = _prep_layer(l, w_in, attn_sink, sgu_g, sgu_w, sgu_b, w_out, ln1_g, ln1_b,
                         peer_wq, peer_keys, peer_u, peer_v, ln2_g, ln2_b)
        xp, kl, vl = _layer(xp, batch, seq, mod[l], 0, batch * seq, lw, None)
        new_k.append(kl.reshape(batch, seq, N_KV, HEAD_DIM))
        new_v.append(vl.reshape(batch, seq, N_KV, HEAD_DIM))
        xs, _, _ = _layer(xs, dec_batch, dec_seq, mod[l], 1, dec_seq, lw, (ck, cv, l, cos_t, sin_t))
    return (xp.reshape(batch, seq, D_MODEL), xs.reshape(dec_batch, dec_seq, D_MODEL),
            jnp.stack(new_k, axis=1), jnp.stack(new_v, axis=1))
```
